```python
import math
import jax, jax.numpy as jnp
from jax import lax
import numpy as np

D_MODEL = 1024
BATCH = 4
SEQ = 8192
DEPTH = 2

HEAD_DIM = 64
ROT_DIM = HEAD_DIM // 4
ROPE_THETA = 500000.0
BLK = 128
NEG_INF = -1e30
EPS = 1e-6
A_HEADS = 8
A_CONFIGS = ((128, 1), (512, 4), (2048, 16))
B_Q_HEADS = 8
B_KV_HEADS = 2
B_GROUP = B_Q_HEADS // B_KV_HEADS
B_WINDOW = 128
C_QK_HEADS = 4
C_V_HEADS = 8
C_DK = 128
C_DV = 128
C_CONV = 4
C_CHUNK = 64
D_FF = 4 * D_MODEL
A_W = A_HEADS * HEAD_DIM
B_QW = B_Q_HEADS * HEAD_DIM
B_KVW = B_KV_HEADS * HEAD_DIM
C_QKW = C_QK_HEADS * C_DK
C_VW = C_V_HEADS * C_DV
IN_WIDTHS = (A_W, A_W, A_W, B_QW, B_KVW, B_KVW, C_QKW, C_QKW, C_VW, C_VW,
             C_V_HEADS, C_V_HEADS, D_MODEL, D_MODEL, D_MODEL)
D_IN = sum(IN_WIDTHS)
MAX_POS_OFFSET = 4096

kernel_name = "hybrid_gated_dilated_swa_deltanet_block"


def rmsnorm(x, gain):
    xf = x.astype(jnp.float32)
    y = xf * lax.rsqrt(jnp.mean(xf * xf, axis=-1, keepdims=True) + EPS)
    return (y * gain.astype(jnp.float32)).astype(x.dtype)


def l2norm(t):
    tf = t.astype(jnp.float32)
    return tf * lax.rsqrt(jnp.sum(tf * tf, axis=-1, keepdims=True) + EPS)


def rope_tables(positions, dtype):
    inv_freq = jnp.power(ROPE_THETA, -jnp.arange(0, ROT_DIM, 2, dtype=jnp.float32) / ROT_DIM)
    ang = positions.astype(jnp.float32)[..., None] * inv_freq
    return jnp.cos(ang)[:, :, None, :].astype(dtype), jnp.sin(ang)[:, :, None, :].astype(dtype)


def apply_rope(x, cos, sin):
    half = ROT_DIM // 2
    x1, x2 = x[..., :half], x[..., half:ROT_DIM]
    return jnp.concatenate([x1 * cos - x2 * sin, x2 * cos + x1 * sin, x[..., ROT_DIM:]], axis=-1)


def split_columns(u):
    outs, start = [], 0
    for width in IN_WIDTHS:
        outs.append(u[..., start:start + width])
        start += width
    return outs


def banded_attention(q, k, v, max_dist, sink=None):
    b, l, hkv, g, d = q.shape
    nb = l // BLK
    qb = q.reshape(b, nb, BLK, hkv, g, d)

    def with_prev(t):
        tb = t.reshape(b, nb, BLK, hkv, d)
        prev = jnp.concatenate([jnp.zeros_like(tb[:, :1]), tb[:, :-1]], axis=1)
        return jnp.concatenate([prev, tb], axis=2)

    kk, vv = with_prev(k), with_prev(v)
    s = jnp.einsum('bnqhgd,bnkhd->bnhgqk', qb, kk,
                   preferred_element_type=jnp.float32) * (d ** -0.5)
    qi = jnp.arange(BLK)[:, None]
    kj = jnp.arange(2 * BLK)[None, :]
    dist = BLK + qi - kj
    band = (dist >= 0) & (dist <= max_dist)
    not_pad = (jnp.arange(nb) > 0)[:, None, None] | (kj >= BLK)[None]
    valid = band[None] & not_pad
    s = jnp.where(valid[None, :, None, None], s, NEG_INF)
    m = jnp.max(s, axis=-1, keepdims=True)
    if sink is not None:
        sk = sink.astype(jnp.float32)[None, None, :, :, None, None]
        m = jnp.maximum(m, sk)
    p = jnp.exp(s - m)
    den = jnp.sum(p, axis=-1, keepdims=True)
    if sink is not None:
        den = den + jnp.exp(sk - m)
    o = jnp.einsum('bnhgqk,bnkhd->bnqhgd', p.astype(v.dtype), vv,
                   preferred_element_type=jnp.float32)
    den_t = jnp.transpose(den[..., 0], (0, 1, 4, 2, 3))
    lse_t = jnp.transpose((m + jnp.log(den))[..., 0], (0, 1, 4, 2, 3))
    o = (o / den_t[..., None]).reshape(b, l, hkv, g, d)
    return o.astype(q.dtype), lse_t.reshape(b, l, hkv, g)


def dilated_attention(q, k, v):
    b, s, h, d = q.shape
    outs, lses = [], []
    for window, dil in A_CONFIGS:
        steps = window // dil
        l = s // dil
        lp = -(-l // BLK) * BLK

        def by_stride(t):
            t = t.reshape(b, l, dil, h, d).transpose(0, 2, 1, 3, 4).reshape(b * dil, l, h, d)
            return jnp.pad(t, ((0, 0), (0, lp - l), (0, 0), (0, 0)))

        o, lse = banded_attention(by_stride(q)[:, :, :, None], by_stride(k), by_stride(v), steps)
        o = o[:, :l, :, 0].reshape(b, dil, l, h, d).transpose(0, 2, 1, 3, 4).reshape(b, s, h, d)
        lse = lse[:, :l, :, 0].reshape(b, dil, l, h).transpose(0, 2, 1, 3).reshape(b, s, h)
        outs.append(o)
        lses.append(lse)
    w = jax.nn.softmax(jnp.stack(lses, axis=0), axis=0)
    o = jnp.sum(w[..., None] * jnp.stack(outs, axis=0).astype(jnp.float32), axis=0)
    return o.astype(q.dtype)


def short_conv(x, w):
    s = x.shape[1]
    xp = jnp.pad(x, ((0, 0), (C_CONV - 1, 0), (0, 0)))
    y = xp[:, 0:s] * w[0]
    for j in range(1, C_CONV):
        y = y + xp[:, j:j + s] * w[j]
    return jax.nn.silu(y)


def gated_delta_rule(q, k, v, g, beta):
    b, s, h, dk = k.shape
    dv = v.shape[-1]
    nc = s // C_CHUNK

    def chunks(t):
        t = t.astype(jnp.float32).reshape((b, nc, C_CHUNK) + t.shape[2:])
        return jnp.moveaxis(t, 3, 1)

    qc = chunks(q) * (dk ** -0.5)
    kc, vc, bc = chunks(k), chunks(v), chunks(beta)
    gc = jnp.cumsum(chunks(g), axis=-1)
    tril = jnp.tril(jnp.ones((C_CHUNK, C_CHUNK), dtype=bool))
    strict = jnp.tril(jnp.ones((C_CHUNK, C_CHUNK), dtype=bool), -1)
    diff = gc[..., :, None] - gc[..., None, :]
    decay = jnp.where(tril, jnp.exp(jnp.where(tril, diff, 0.0)), 0.0)
    kkt = jnp.einsum('bhncd,bhnjd->bhncj', kc, kc)
    a_mat = jnp.where(strict, kkt * decay * bc[..., None], 0.0)
    eye = jnp.eye(C_CHUNK, dtype=jnp.float32)
    rhs = jnp.concatenate([vc * bc[..., None], kc * (bc * jnp.exp(gc))[..., None]], axis=-1)
    sol = lax.linalg.triangular_solve(a_mat + eye, rhs, left_side=True, lower=True,
                                      unit_diagonal=True)
    u, w = sol[..., :dv], sol[..., dv:]
    attn = jnp.where(tril, jnp.einsum('bhncd,bhnjd->bhncj', qc, kc) * decay, 0.0)
    q_dec = qc * jnp.exp(gc)[..., None]
    k_dec = kc * jnp.exp(gc[..., -1:] - gc)[..., None]
    g_last = jnp.exp(gc[..., -1])

    def step(state, xs):
        u_i, w_i, attn_i, qd_i, kd_i, gl_i = xs
        v_new = u_i - jnp.einsum('bhcd,bhde->bhce', w_i, state)
        o_i = (jnp.einsum('bhcd,bhde->bhce', qd_i, state)
               + jnp.einsum('bhcj,bhje->bhce', attn_i, v_new))
        state = state * gl_i[..., None, None] + jnp.einsum('bhcd,bhce->bhde', kd_i, v_new)
        return state, o_i

    xs = tuple(jnp.moveaxis(t, 2, 0) for t in (u, w, attn, q_dec, k_dec, g_last))
    state0 = jnp.zeros((b, h, dk, dv), jnp.float32)
    _, o = lax.scan(step, state0, xs)
    return jnp.transpose(o, (1, 0, 3, 2, 4)).reshape(b, s, h, dv)


def mixer_block(h, cos, sin, w_in, b_in, conv_w, a_log, dt_bias, sinks, c_norm,
                w_branch_a, w_branch_b, w_branch_c, w_out):
    b, s, _ = h.shape
    u = h @ w_in + b_in
    (a_q, a_k, a_v, b_q, b_k, b_v, c_q, c_k, c_v, c_z, c_a, c_b,
     gate_a, gate_b, gate_c) = split_columns(u)

    def heads(t, n):
        return t.reshape(b, s, n, -1)

    ya = dilated_attention(apply_rope(heads(a_q, A_HEADS), cos, sin),
                           apply_rope(heads(a_k, A_HEADS), cos, sin),
                           heads(a_v, A_HEADS)).reshape(b, s, A_W)

    qb = apply_rope(heads(b_q, B_Q_HEADS), cos, sin).reshape(b, s, B_KV_HEADS, B_GROUP, HEAD_DIM)
    kb = apply_rope(heads(b_k, B_KV_HEADS), cos, sin)
    vb = heads(b_v, B_KV_HEADS)
    yb, _ = banded_attention(qb, kb, vb, B_WINDOW - 1, sink=sinks.reshape(B_KV_HEADS, B_GROUP))
    yb = yb.reshape(b, s, B_QW)

    qkv = short_conv(jnp.concatenate([c_q, c_k, c_v], axis=-1), conv_w)
    rep = C_V_HEADS // C_QK_HEADS
    cq = jnp.repeat(l2norm(qkv[..., :C_QKW].reshape(b, s, C_QK_HEADS, C_DK)), rep, axis=2)
    ck = jnp.repeat(l2norm(qkv[..., C_QKW:2 * C_QKW].reshape(b, s, C_QK_HEADS, C_DK)), rep, axis=2)
    cv = qkv[..., 2 * C_QKW:].reshape(b, s, C_V_HEADS, C_DV)
    beta = jax.nn.sigmoid(c_b.astype(jnp.float32))
    g = -jnp.exp(a_log.astype(jnp.float32)) * jax.nn.softplus(
        c_a.astype(jnp.float32) + dt_bias.astype(jnp.float32))
    yc = gated_delta_rule(cq, ck, cv, g, beta)
    yc = rmsnorm(yc, c_norm) * jax.nn.silu(c_z.reshape(b, s, C_V_HEADS, C_DV).astype(jnp.float32))
    yc = yc.astype(h.dtype).reshape(b, s, C_VW)

    merged = (jax.nn.sigmoid(gate_a) * (ya @ w_branch_a)
              + jax.nn.sigmoid(gate_b) * (yb @ w_branch_b)
              + jax.nn.sigmoid(gate_c) * (yc @ w_branch_c))
    return merged @ w_out


def setup_inputs(seed: int = 0) -> dict:
    key = jax.random.key(seed)
    ks = jax.random.split(key, 20)
    f32 = jnp.float32

    def dense(k, shape, fan_in, scale=1.0):
        return jax.random.normal(k, shape, f32) * (scale * fan_in ** -0.5)

    def gain(k, shape):
        return 1.0 + 0.05 * jax.random.normal(k, shape, f32)

    res_scale = (2 * DEPTH) ** -0.5
    x = jax.random.normal(ks[0], (BATCH, SEQ, D_MODEL), f32)
    positions = (jax.random.randint(ks[1], (BATCH, 1), 0, MAX_POS_OFFSET, dtype=jnp.int32)
                 + jnp.arange(SEQ, dtype=jnp.int32)[None, :])
    norm_mix = gain(ks[2], (DEPTH, D_MODEL))
    w_in = dense(ks[3], (DEPTH, D_MODEL, D_IN), D_MODEL)
    b_in = 0.02 * jax.random.normal(ks[4], (DEPTH, D_IN), f32)
    conv_w = dense(ks[5], (DEPTH, C_CONV, 2 * C_QKW + C_VW), C_CONV)
    a_log = jnp.log(jax.random.uniform(ks[6], (DEPTH, C_V_HEADS), f32, 1.0, 16.0))
    dt = jnp.exp(jax.random.uniform(ks[7], (DEPTH, C_V_HEADS), f32, math.log(1e-3), math.log(1e-1)))
    dt_bias = dt + jnp.log(-jnp.expm1(-dt))
    sinks = 0.5 * jax.random.normal(ks[8], (DEPTH, B_Q_HEADS), f32)
    c_norm = gain(ks[9], (DEPTH, C_DV))
    w_branch_a = dense(ks[10], (DEPTH, A_W, D_MODEL), A_W)
    w_branch_b = dense(ks[11], (DEPTH, B_QW, D_MODEL), B_QW)
    w_branch_c = dense(ks[12], (DEPTH, C_VW, D_MODEL), C_VW)
    w_out = dense(ks[13], (DEPTH, D_MODEL, D_MODEL), D_MODEL, res_scale)
    norm_ffn = gain(ks[14], (DEPTH, D_MODEL))
    w_ff1 = dense(ks[15], (DEPTH, D_MODEL, D_FF), D_MODEL)
    w_ff2 = dense(ks[16], (DEPTH, D_FF, D_MODEL), D_FF, res_scale)
    norm_final = gain(ks[17], (D_MODEL,))
    return {"x": x, "positions": positions, "norm_mix": norm_mix, "w_in": w_in, "b_in": b_in,
            "conv_w": conv_w, "a_log": a_log, "dt_bias": dt_bias, "sinks": sinks, "c_norm": c_norm,
            "w_branch_a": w_branch_a, "w_branch_b": w_branch_b, "w_branch_c": w_branch_c,
            "w_out": w_out, "norm_ffn": norm_ffn, "w_ff1": w_ff1, "w_ff2": w_ff2,
            "norm_final": norm_final}


def reference(x, positions, norm_mix, w_in, b_in, conv_w, a_log, dt_bias, sinks, c_norm,
              w_branch_a, w_branch_b, w_branch_c, w_out, norm_ffn, w_ff1, w_ff2, norm_final):
    cos, sin = rope_tables(positions, x.dtype)
    for layer in range(DEPTH):
        h = rmsnorm(x, norm_mix[layer])
        x = x + mixer_block(h, cos, sin, w_in[layer], b_in[layer], conv_w[layer], a_log[layer],
                            dt_bias[layer], sinks[layer], c_norm[layer], w_branch_a[layer],
                            w_branch_b[layer], w_branch_c[layer], w_out[layer])
        h = rmsnorm(x, norm_ffn[layer])
        x = x + jnp.square(jax.nn.relu(h @ w_ff1[layer])) @ w_ff2[layer]
    return rmsnorm(x, norm_final)
```

```python
import functools
import math

import jax
import jax.numpy as jnp
from jax import lax
from jax.experimental import pallas as pl
from jax.experimental.pallas import tpu as pltpu

F32 = jnp.float32
BF16 = jnp.bfloat16

D_MODEL = 1024
HEAD_DIM = 64
ROT_DIM = 16
ROPE_THETA = 500000.0
BLK = 128
NEG_INF = -1e30
EPS = 1e-6
A_CONFIGS = ((128, 1), (512, 4), (2048, 16))
B_WINDOW = 128
C_V_HEADS = 8
C_DK = 128
C_DV = 128
C_CONV = 4
C_CHUNK = 64
D_FF = 4096

LANES = 128
UA_W = 1536
UB_W = 768
UC_W = 3072
UG_W = 3072
SM_W = 128
W_ALL = UA_W + UB_W + UC_W + UG_W + SM_W
VMEM_LIMIT = 56 * 1024 * 1024


def _cparams(sem):
    return pltpu.CompilerParams(dimension_semantics=sem, vmem_limit_bytes=VMEM_LIMIT)


def _rms(x, gain):
    return x * lax.rsqrt(jnp.mean(x * x, axis=-1, keepdims=True) + EPS) * gain


def _inproj_kernel(x_ref, g_ref, w_ref, b_ref, cos_ref, s1_ref, s2_ref,
                   ua_ref, ub_ref, uc_ref, ug_ref, sm_ref):
    h = _rms(x_ref[...], g_ref[...]).astype(BF16)
    cosv, s1, s2 = cos_ref[...], s1_ref[...], s2_ref[...]

    def seg(off, width):
        return (jnp.dot(h, w_ref[:, off:off + width], preferred_element_type=F32)
                + b_ref[:, off:off + width])

    def rope(a, scale):
        parts = []
        for g in range(a.shape[1] // LANES):
            t = a[:, g * LANES:(g + 1) * LANES]
            r = t * cosv + pltpu.roll(t, LANES - 8, 1) * s1 + pltpu.roll(t, 8, 1) * s2
            parts.append(r * scale if scale != 1.0 else r)
        return jnp.concatenate(parts, axis=1)

    qscale = HEAD_DIM ** -0.5
    ua_ref[:, 0:512] = rope(seg(0, 512), qscale).astype(BF16)
    ua_ref[:, 512:1024] = rope(seg(512, 512), 1.0).astype(BF16)
    ua_ref[:, 1024:1536] = seg(1024, 512).astype(BF16)
    ub_ref[:, 0:512] = rope(seg(1536, 512), qscale).astype(BF16)
    ub_ref[:, 512:640] = rope(seg(2048, 128), 1.0).astype(BF16)
    ub_ref[:, 640:768] = seg(2176, 128).astype(BF16)
    off = UA_W + UB_W
    for c in range(4):
        uc_ref[:, c * 512:(c + 1) * 512] = seg(off + c * 512, 512).astype(BF16)
    for c in range(4, 6):
        z = seg(off + c * 512, 512)
        uc_ref[:, c * 512:(c + 1) * 512] = (z * jax.nn.sigmoid(z)).astype(BF16)
    off += UC_W
    for c in range(6):
        ug_ref[:, c * 512:(c + 1) * 512] = jax.nn.sigmoid(seg(off + c * 512, 512)).astype(BF16)
    off += UG_W
    sm_ref[...] = seg(off, SM_W)


def _inproj(x2, gain, w_all, b_all, cosv, s1, s2, tm):
    m = x2.shape[0]
    row = lambda i: (i, 0)
    const = lambda i: (0, 0)
    return pl.pallas_call(
        _inproj_kernel,
        grid=(m // tm,),
        in_specs=[
            pl.BlockSpec((tm, D_MODEL), row),
            pl.BlockSpec((1, D_MODEL), const),
            pl.BlockSpec((D_MODEL, W_ALL), const),
            pl.BlockSpec((1, W_ALL), const),
            pl.BlockSpec((tm, LANES), row),
            pl.BlockSpec((tm, LANES), row),
            pl.BlockSpec((tm, LANES), row),
        ],
        out_specs=[
            pl.BlockSpec((tm, UA_W), row),
            pl.BlockSpec((tm, UB_W), row),
            pl.BlockSpec((tm, UC_W), row),
            pl.BlockSpec((tm, UG_W), row),
            pl.BlockSpec((tm, SM_W), row),
        ],
        out_shape=[
            jax.ShapeDtypeStruct((m, UA_W), BF16),
            jax.ShapeDtypeStruct((m, UB_W), BF16),
            jax.ShapeDtypeStruct((m, UC_W), BF16),
            jax.ShapeDtypeStruct((m, UG_W), BF16),
            jax.ShapeDtypeStruct((m, SM_W), F32),
        ],
        compiler_params=_cparams(("parallel",)),
        name="inproj",
    )(x2, gain, w_all, b_all, cosv, s1, s2)


def _band_valid(jb, max_dist):
    qi = lax.broadcasted_iota(jnp.int32, (BLK, 2 * BLK), 0)
    kj = lax.broadcasted_iota(jnp.int32, (BLK, 2 * BLK), 1)
    dist = BLK + qi - kj
    band = (dist >= 0) & (dist <= max_dist)
    return band & ((jb > 0) | (kj >= BLK))


def _attend_pair(qp, kcat, vcat, valid, sinks):
    lane = lax.broadcasted_iota(jnp.int32, (BLK, LANES), 1)
    first = lane < HEAD_DIM
    outs, lses = [], []
    for hh in range(2):
        qm = jnp.where(first if hh == 0 else ~first, qp, jnp.zeros_like(qp))
        s = lax.dot_general(qm, kcat, (((1,), (1,)), ((), ())), preferred_element_type=F32)
        s = jnp.where(valid, s, NEG_INF)
        m = jnp.max(s, axis=-1, keepdims=True)
        if sinks is not None:
            m = jnp.maximum(m, sinks[hh])
        p = jnp.exp(s - m)
        den = jnp.sum(p, axis=-1, keepdims=True)
        if sinks is not None:
            den = den + jnp.exp(sinks[hh] - m)
        o = jnp.dot(p.astype(BF16), vcat, preferred_element_type=F32)
        outs.append(o / den)
        lses.append(m + jnp.log(den))
    return jnp.where(first, outs[0], outs[1]), jnp.where(first, lses[0], lses[1])


def _attn_a_kernel(q_ref, kc_ref, kp_ref, vc_ref, vp_ref, o_ref, lse_ref, *, max_dist):
    valid = _band_valid(pl.program_id(2), max_dist)
    for p in range(4):
        sl = slice(p * LANES, (p + 1) * LANES)
        kcat = jnp.concatenate([kp_ref[0, :, sl], kc_ref[0, :, sl]], axis=0)
        vcat = jnp.concatenate([vp_ref[0, :, sl], vc_ref[0, :, sl]], axis=0)
        o, lse = _attend_pair(q_ref[0, :, sl], kcat, vcat, valid, None)
        o_ref[0, :, sl] = o.astype(BF16)
        lse_ref[0, :, sl] = lse


def _attn_a(ua, bsz, seq, window, dil):
    l = seq // dil
    steps = window // dil
    view = ua.reshape(bsz, l, dil * UA_W)
    blk = (1, BLK, 512)
    cur = lambda c: (lambda b, r, j: (b, j, 3 * r + c))
    prev = lambda c: (lambda b, r, j: (b, jnp.maximum(j - 1, 0), 3 * r + c))
    out = lambda b, r, j: (b, j, r)
    o, lse = pl.pallas_call(
        functools.partial(_attn_a_kernel, max_dist=steps),
        grid=(bsz, dil, l // BLK),
        in_specs=[pl.BlockSpec(blk, cur(0)), pl.BlockSpec(blk, cur(1)), pl.BlockSpec(blk, prev(1)),
                  pl.BlockSpec(blk, cur(2)), pl.BlockSpec(blk, prev(2))],
        out_specs=[pl.BlockSpec(blk, out), pl.BlockSpec(blk, out)],
        out_shape=[jax.ShapeDtypeStruct((bsz, l, dil * 512), BF16),
                   jax.ShapeDtypeStruct((bsz, l, dil * 512), F32)],
        compiler_params=_cparams(("parallel", "parallel", "arbitrary")),
        name=f"attn_a_d{dil}",
    )(view, view, view, view, view)
    return o.reshape(bsz * seq, 512), lse.reshape(bsz * seq, 512)


def _attn_b_kernel(sink_ref, q_ref, kc_ref, kp_ref, vc_ref, vp_ref, o_ref):
    valid = _band_valid(pl.program_id(1), B_WINDOW - 1)
    lane = lax.broadcasted_iota(jnp.int32, (2 * BLK, LANES), 1)
    first = lane < HEAD_DIM
    k = jnp.concatenate([kp_ref[0], kc_ref[0]], axis=0)
    v = jnp.concatenate([vp_ref[0], vc_ref[0]], axis=0)
    k_sw = pltpu.roll(k, HEAD_DIM, 1)
    v_sw = pltpu.roll(v, HEAD_DIM, 1)
    kdup = (jnp.where(first, k, k_sw), jnp.where(first, k_sw, k))
    vdup = (jnp.where(first, v, v_sw), jnp.where(first, v_sw, v))
    for p in range(4):
        sl = slice(p * LANES, (p + 1) * LANES)
        kv = p // 2
        sinks = (sink_ref[2 * p], sink_ref[2 * p + 1])
        o, _ = _attend_pair(q_ref[0, :, sl], kdup[kv], vdup[kv], valid, sinks)
        o_ref[0, :, sl] = o.astype(BF16)


def _attn_b(ub, sinks, bsz, seq):
    view = ub.reshape(bsz, seq, UB_W)
    kvblk = (1, BLK, LANES)
    cur = lambda c: (lambda b, j: (b, j, c))
    prev = lambda c: (lambda b, j: (b, jnp.maximum(j - 1, 0), c))
    o = pl.pallas_call(
        _attn_b_kernel,
        grid=(bsz, seq // BLK),
        in_specs=[pl.BlockSpec(memory_space=pltpu.SMEM),
                  pl.BlockSpec((1, BLK, 512), lambda b, j: (b, j, 0)),
                  pl.BlockSpec(kvblk, cur(4)), pl.BlockSpec(kvblk, prev(4)),
                  pl.BlockSpec(kvblk, cur(5)), pl.BlockSpec(kvblk, prev(5))],
        out_specs=pl.BlockSpec((1, BLK, 512), lambda b, j: (b, j, 0)),
        out_shape=jax.ShapeDtypeStruct((bsz, seq, 512), BF16),
        compiler_params=_cparams(("parallel", "arbitrary")),
        name="attn_b",
    )(sinks, view, view, view, view, view)
    return o.reshape(bsz * seq, 512)


def _split3(a):
    hi = a.astype(BF16)
    r = a - hi.astype(F32)
    mid = r.astype(BF16)
    lo = (r - mid.astype(F32)).astype(BF16)
    return hi, mid, lo


def _dot_nt(a, b):
    return lax.dot_general(a, b, (((1,), (1,)), ((), ())), preferred_element_type=F32)


def _dot_tn(a, b):
    return lax.dot_general(a, b, (((0,), (0,)), ((), ())), preferred_element_type=F32)


def _mm(a, b):
    return jnp.dot(a.astype(BF16), b.astype(BF16), preferred_element_type=F32)


def _gdn_kernel(cq_ref, ck_ref, cv_ref, zs_ref, sm_ref, cw_ref, alog_ref, dtb_ref, cn_ref, e_ref,
                y_ref, state_ref, ext_ref):
    c = C_CHUNK

    @pl.when(pl.program_id(1) == 0)
    def _():
        state_ref[...] = jnp.zeros_like(state_ref)
        ext_ref[0:8, :] = jnp.zeros((8, 2048), F32)

    ext_ref[8:8 + c, 0:512] = cq_ref[...].astype(F32)
    ext_ref[8:8 + c, 512:1024] = ck_ref[...].astype(F32)
    ext_ref[8:8 + c, 1024:2048] = cv_ref[...].astype(F32)
    y = ext_ref[5:5 + c, :] * cw_ref[0:1, :]
    for j in range(1, C_CONV):
        y = y + ext_ref[5 + j:5 + j + c, :] * cw_ref[j:j + 1, :]
    ext_ref[0:8, :] = ext_ref[c:c + 8, :]
    y = y * jax.nn.sigmoid(y)

    def l2n(t):
        return t * lax.rsqrt(jnp.sum(t * t, axis=-1, keepdims=True) + EPS)

    sm = sm_ref[...]
    lane = lax.broadcasted_iota(jnp.int32, (c, LANES), 1)
    z = sm + dtb_ref[...]
    softplus = jnp.maximum(z, 0.0) + jnp.log1p(jnp.exp(-jnp.abs(z)))
    g = -jnp.exp(alog_ref[...]) * softplus
    beta = jax.nn.sigmoid(sm)
    ri = lax.broadcasted_iota(jnp.int32, (c, c), 0)
    ci = lax.broadcasted_iota(jnp.int32, (c, c), 1)
    tril = ri >= ci
    strict = ri > ci
    ltri = tril.astype(BF16)
    g_hi, g_mid, g_lo = _split3(g)
    gc = (jnp.dot(ltri, g_hi, preferred_element_type=F32) + jnp.dot(ltri, g_mid, preferred_element_type=F32)
          + jnp.dot(ltri, g_lo, preferred_element_type=F32))
    comb = jnp.where(lane < C_V_HEADS, gc, beta)
    c_hi, c_mid, c_lo = _split3(comb)
    e = e_ref[...]
    expd = (jnp.dot(c_hi, e, preferred_element_type=F32) + jnp.dot(c_mid, e, preferred_element_type=F32)
            + jnp.dot(c_lo, e, preferred_element_type=F32))
    cn = cn_ref[...]

    for h in range(C_V_HEADS):
        p = h // 2
        q_h = l2n(y[:, p * C_DK:(p + 1) * C_DK]) * (C_DK ** -0.5)
        k_h = l2n(y[:, 512 + p * C_DK:512 + (p + 1) * C_DK])
        v_h = y[:, 1024 + h * C_DV:1024 + (h + 1) * C_DV]
        ge = expd[:, h * LANES:(h + 1) * LANES]
        be = expd[:, 1024 + h * LANES:1024 + (h + 1) * LANES]
        eg = jnp.exp(ge)
        glast = ge[c - 1:c, :]
        k16 = k_h.astype(BF16)
        kk = _dot_nt(k16, k16)
        qk = _dot_nt(q_h.astype(BF16), k16)
        diff = ge[:, 0:c] - ge.T[0:c, :]
        dec = jnp.where(tril, jnp.exp(jnp.where(tril, diff, 0.0)), 0.0)
        a_mat = jnp.where(strict, kk * dec * be[:, 0:c], 0.0)
        attn = jnp.where(tril, qk * dec, 0.0)
        x = jnp.concatenate([v_h * be, k_h * be * eg], axis=1)
        pw = -a_mat
        for i in range(6):
            x = x + _mm(pw, x)
            if i < 5:
                pw = _mm(pw, pw)
        u = x[:, 0:C_DV]
        w = x[:, C_DV:]
        s = state_ref[h]
        s16 = s.astype(BF16)
        v_new = u - jnp.dot(w.astype(BF16), s16, preferred_element_type=F32)
        o = jnp.dot((q_h * eg).astype(BF16), s16, preferred_element_type=F32) + _mm(attn, v_new)
        k_dec = k_h * jnp.exp(glast - ge)
        state_ref[h] = s * jnp.exp(glast) + _dot_tn(k_dec.astype(BF16), v_new.astype(BF16))
        sl = slice(h * C_DV, (h + 1) * C_DV)
        y_ref[:, sl] = (_rms(o, cn) * zs_ref[:, sl].astype(F32)).astype(BF16)


def _gdn(uc, sm, conv_w, a_log, dt_bias, c_norm, bsz, seq):
    c = C_CHUNK
    nc = seq // c
    pad = lambda v: jnp.pad(v.astype(F32), (0, LANES - v.shape[0])).reshape(1, LANES)
    li = jnp.arange(LANES)[:, None]
    cj = jnp.arange(2048)[None, :]
    e = (((li < 8) & (cj < 1024) & (cj // LANES == li))
         | ((li >= 8) & (li < 16) & (cj >= 1024) & ((cj - 1024) // LANES == li - 8))).astype(BF16)
    row = lambda blockcol: (lambda b, t: (b * nc + t, blockcol))
    const = lambda b, t: (0, 0)
    return pl.pallas_call(
        _gdn_kernel,
        grid=(bsz, nc),
        in_specs=[
            pl.BlockSpec((c, 512), row(0)),
            pl.BlockSpec((c, 512), row(1)),
            pl.BlockSpec((c, 1024), row(1)),
            pl.BlockSpec((c, 1024), row(2)),
            pl.BlockSpec((c, SM_W), row(0)),
            pl.BlockSpec((C_CONV, 2048), const),
            pl.BlockSpec((1, LANES), const),
            pl.BlockSpec((1, LANES), const),
            pl.BlockSpec((1, LANES), const),
            pl.BlockSpec((LANES, 2048), const),
        ],
        out_specs=pl.BlockSpec((c, 1024), row(0)),
        out_shape=jax.ShapeDtypeStruct((bsz * seq, 1024), BF16),
        scratch_shapes=[pltpu.VMEM((C_V_HEADS, C_DK, C_DV), F32), pltpu.VMEM((c + 8, 2048), F32)],
        compiler_params=_cparams(("parallel", "arbitrary")),
        name="gdn",
    )(uc, uc, uc, uc, sm, conv_w.astype(F32), pad(a_log), pad(dt_bias), c_norm.astype(F32).reshape(1, LANES), e)


def _merge_kernel(x_ref, o1_ref, o2_ref, o3_ref, l1_ref, l2_ref, l3_ref, yb_ref, yc_ref, ug_ref,
                  wa_ref, wb_ref, wc_ref, wo_ref, out_ref):
    l1, l2, l3 = l1_ref[...], l2_ref[...], l3_ref[...]
    m = jnp.maximum(jnp.maximum(l1, l2), l3)
    e1, e2, e3 = jnp.exp(l1 - m), jnp.exp(l2 - m), jnp.exp(l3 - m)
    ya = (e1 * o1_ref[...].astype(F32) + e2 * o2_ref[...].astype(F32) + e3 * o3_ref[...].astype(F32)) / (e1 + e2 + e3)
    merged = (ug_ref[:, 0:1024].astype(F32) * jnp.dot(ya.astype(BF16), wa_ref[...], preferred_element_type=F32)
              + ug_ref[:, 1024:2048].astype(F32) * jnp.dot(yb_ref[...], wb_ref[...], preferred_element_type=F32)
              + ug_ref[:, 2048:3072].astype(F32) * jnp.dot(yc_ref[...], wc_ref[...], preferred_element_type=F32))
    out_ref[...] = x_ref[...] + jnp.dot(merged.astype(BF16), wo_ref[...], preferred_element_type=F32)


def _merge(x2, o_list, lse_list, yb, yc, ug, wa, wb, wc, wo, tm):
    m = x2.shape[0]
    row = lambda i: (i, 0)
    const = lambda i: (0, 0)
    rs = lambda w: pl.BlockSpec((tm, w), row)
    return pl.pallas_call(
        _merge_kernel,
        grid=(m // tm,),
        in_specs=[rs(D_MODEL), rs(512), rs(512), rs(512), rs(512), rs(512), rs(512), rs(512), rs(1024), rs(UG_W),
                  pl.BlockSpec((512, D_MODEL), const), pl.BlockSpec((512, D_MODEL), const),
                  pl.BlockSpec((1024, D_MODEL), const), pl.BlockSpec((D_MODEL, D_MODEL), const)],
        out_specs=rs(D_MODEL),
        out_shape=jax.ShapeDtypeStruct((m, D_MODEL), F32),
        compiler_params=_cparams(("parallel",)),
        name="merge",
    )(x2, *o_list, *lse_list, yb, yc, ug, wa, wb, wc, wo)


def _ffn_kernel(x_ref, g_ref, w1_ref, w2_ref, gf_ref, out_ref, *, final_norm):
    x = x_ref[...]
    h = _rms(x, g_ref[...]).astype(BF16)
    acc = x
    for c in range(D_FF // 1024):
        sl = slice(c * 1024, (c + 1) * 1024)
        a = jnp.maximum(jnp.dot(h, w1_ref[:, sl], preferred_element_type=F32), 0.0)
        acc = acc + jnp.dot((a * a).astype(BF16), w2_ref[sl, :], preferred_element_type=F32)
    out_ref[...] = _rms(acc, gf_ref[...]) if final_norm else acc


def _ffn(x2, gain, w1, w2, gain_final, final_norm, tm):
    m = x2.shape[0]
    row = lambda i: (i, 0)
    const = lambda i: (0, 0)
    return pl.pallas_call(
        functools.partial(_ffn_kernel, final_norm=final_norm),
        grid=(m // tm,),
        in_specs=[pl.BlockSpec((tm, D_MODEL), row), pl.BlockSpec((1, D_MODEL), const),
                  pl.BlockSpec((D_MODEL, D_FF), const), pl.BlockSpec((D_FF, D_MODEL), const),
                  pl.BlockSpec((1, D_MODEL), const)],
        out_specs=pl.BlockSpec((tm, D_MODEL), row),
        out_shape=jax.ShapeDtypeStruct((m, D_MODEL), F32),
        compiler_params=_cparams(("parallel",)),
        name="ffn",
    )(x2, gain, w1, w2, gain_final)


def _rope_tables(positions):
    half = ROT_DIM // 2
    inv_freq = jnp.power(ROPE_THETA, -jnp.arange(0, ROT_DIM, 2, dtype=F32) / ROT_DIM)
    ang = positions.astype(F32).reshape(-1, 1) * inv_freq
    cos, sin = jnp.cos(ang), jnp.sin(ang)
    n = ang.shape[0]
    rest = HEAD_DIM - ROT_DIM
    cos64 = jnp.concatenate([cos, cos, jnp.ones((n, rest), F32)], axis=1)
    s1 = jnp.concatenate([-sin, jnp.zeros((n, half + rest), F32)], axis=1)
    s2 = jnp.concatenate([jnp.zeros((n, half), F32), sin, jnp.zeros((n, rest), F32)], axis=1)
    two = lambda t: jnp.concatenate([t, t], axis=1)
    return two(cos64), two(s1), two(s2)


def kernel(x, positions, norm_mix, w_in, b_in, conv_w, a_log, dt_bias, sinks, c_norm, w_branch_a, w_branch_b,
           w_branch_c, w_out, norm_ffn, w_ff1, w_ff2, norm_final):
    bsz, seq, d = x.shape
    depth = w_in.shape[0]
    assert d == D_MODEL and seq % (16 * BLK) == 0
    m = bsz * seq
    cosv, s1, s2 = _rope_tables(positions)
    x2 = x.reshape(m, d)
    ca = 5376
    for layer in range(depth):
        wl, bl = w_in[layer], b_in[layer]
        w_all = jnp.concatenate([wl[:, :ca], wl[:, ca + 16:], wl[:, ca:ca + 16],
                                 jnp.zeros((d, SM_W - 16), F32)], axis=1).astype(BF16)
        b_all = jnp.concatenate([bl[:ca], bl[ca + 16:], bl[ca:ca + 16], jnp.zeros((SM_W - 16,), F32)]).reshape(1, -1)
        ua, ub, uc, ug, sm = _inproj(x2, norm_mix[layer].reshape(1, d), w_all, b_all, cosv, s1, s2, tm=256)
        o_list, lse_list = [], []
        for window, dil in A_CONFIGS:
            o, lse = _attn_a(ua, bsz, seq, window, dil)
            o_list.append(o)
            lse_list.append(lse)
        yb = _attn_b(ub, sinks[layer].astype(F32), bsz, seq)
        yc = _gdn(uc, sm, conv_w[layer], a_log[layer], dt_bias[layer], c_norm[layer], bsz, seq)
        x2 = _merge(x2, o_list, lse_list, yb, yc, ug, w_branch_a[layer].astype(BF16), w_branch_b[layer].astype(BF16),
                    w_branch_c[layer].astype(BF16), w_out[layer].astype(BF16), tm=512)
        x2 = _ffn(x2, norm_ffn[layer].reshape(1, d), w_ff1[layer].astype(BF16), w_ff2[layer].astype(BF16),
                  norm_final.reshape(1, d), final_norm=(layer == depth - 1), tm=512)
    return x2.reshape(bsz, seq, d)
```

```python
import functools
import math

import jax
import jax.numpy as jnp
from jax import lax
from jax.experimental import pallas as pl
from jax.experimental.pallas import tpu as pltpu

F32 = jnp.float32
BF16 = jnp.bfloat16

D_MODEL = 1024
HEAD_DIM = 64
ROT_DIM = 16
ROPE_THETA = 500000.0
BLK = 128
NEG_INF = -1e30
EPS = 1e-6
A_CONFIGS = ((128, 1), (512, 4), (2048, 16))
B_WINDOW = 128
C_V_HEADS = 8
C_DK = 128
C_DV = 128
C_CONV = 4
C_CHUNK = 64
D_FF = 4096

LANES = 128
UA_W = 1536
UB_W = 768
UC_W = 3072
UG_W = 3072
SM_W = 128
W_ALL = UA_W + UB_W + UC_W + UG_W + SM_W
VMEM_LIMIT = 56 * 1024 * 1024


def _cparams(sem):
    return pltpu.CompilerParams(dimension_semantics=sem, vmem_limit_bytes=VMEM_LIMIT)


def _rms(x, gain):
    return x * lax.rsqrt(jnp.mean(x * x, axis=-1, keepdims=True) + EPS) * gain


def _inproj_kernel(x_ref, g_ref, w_ref, b_ref, cos_ref, s1_ref, s2_ref,
                   ua_ref, ub_ref, uc_ref, ug_ref, sm_ref):
    h = _rms(x_ref[...], g_ref[...]).astype(BF16)
    cosv, s1, s2 = cos_ref[...], s1_ref[...], s2_ref[...]

    def seg(off, width):
        return (jnp.dot(h, w_ref[:, off:off + width], preferred_element_type=F32)
                + b_ref[:, off:off + width])

    def rope(a, scale):
        parts = []
        for g in range(a.shape[1] // LANES):
            t = a[:, g * LANES:(g + 1) * LANES]
            r = t * cosv + pltpu.roll(t, LANES - 8, 1) * s1 + pltpu.roll(t, 8, 1) * s2
            parts.append(r * scale if scale != 1.0 else r)
        return jnp.concatenate(parts, axis=1)

    qscale = HEAD_DIM ** -0.5
    ua_ref[:, 0:512] = rope(seg(0, 512), qscale).astype(BF16)
    ua_ref[:, 512:1024] = rope(seg(512, 512), 1.0).astype(BF16)
    ua_ref[:, 1024:1536] = seg(1024, 512).astype(BF16)
    ub_ref[:, 0:512] = rope(seg(1536, 512), qscale).astype(BF16)
    ub_ref[:, 512:640] = rope(seg(2048, 128), 1.0).astype(BF16)
    ub_ref[:, 640:768] = seg(2176, 128).astype(BF16)
    off = UA_W + UB_W
    for c in range(4):
        uc_ref[:, c * 512:(c + 1) * 512] = seg(off + c * 512, 512).astype(BF16)
    for c in range(4, 6):
        z = seg(off + c * 512, 512)
        uc_ref[:, c * 512:(c + 1) * 512] = (z * jax.nn.sigmoid(z)).astype(BF16)
    off += UC_W
    for c in range(6):
        ug_ref[:, c * 512:(c + 1) * 512] = jax.nn.sigmoid(seg(off + c * 512, 512)).astype(BF16)
    off += UG_W
    sm_ref[...] = seg(off, SM_W)


def _inproj(x2, gain, w_all, b_all, cosv, s1, s2, tm):
    m = x2.shape[0]
    row = lambda i: (i, 0)
    const = lambda i: (0, 0)
    return pl.pallas_call(
        _inproj_kernel,
        grid=(m // tm,),
        in_specs=[
            pl.BlockSpec((tm, D_MODEL), row),
            pl.BlockSpec((1, D_MODEL), const),
            pl.BlockSpec((D_MODEL, W_ALL), const),
            pl.BlockSpec((1, W_ALL), const),
            pl.BlockSpec((tm, LANES), row),
            pl.BlockSpec((tm, LANES), row),
            pl.BlockSpec((tm, LANES), row),
        ],
        out_specs=[
            pl.BlockSpec((tm, UA_W), row),
            pl.BlockSpec((tm, UB_W), row),
            pl.BlockSpec((tm, UC_W), row),
            pl.BlockSpec((tm, UG_W), row),
            pl.BlockSpec((tm, SM_W), row),
        ],
        out_shape=[
            jax.ShapeDtypeStruct((m, UA_W), BF16),
            jax.ShapeDtypeStruct((m, UB_W), BF16),
            jax.ShapeDtypeStruct((m, UC_W), BF16),
            jax.ShapeDtypeStruct((m, UG_W), BF16),
            jax.ShapeDtypeStruct((m, SM_W), F32),
        ],
        compiler_params=_cparams(("parallel",)),
        name="inproj",
    )(x2, gain, w_all, b_all, cosv, s1, s2)


def _band_valid(jb, max_dist):
    qi = lax.broadcasted_iota(jnp.int32, (BLK, 2 * BLK), 0)
    kj = lax.broadcasted_iota(jnp.int32, (BLK, 2 * BLK), 1)
    dist = BLK + qi - kj
    band = (dist >= 0) & (dist <= max_dist)
    return band & ((jb > 0) | (kj >= BLK))


def _attend_pair(qp, kcat, vcat, valid, sinks):
    lane = lax.broadcasted_iota(jnp.int32, (BLK, LANES), 1)
    first = lane < HEAD_DIM
    outs, lses = [], []
    for hh in range(2):
        qm = jnp.where(first if hh == 0 else ~first, qp, jnp.zeros_like(qp))
        s = lax.dot_general(qm, kcat, (((1,), (1,)), ((), ())), preferred_element_type=F32)
        s = jnp.where(valid, s, NEG_INF)
        m = jnp.max(s, axis=-1, keepdims=True)
        if sinks is not None:
            m = jnp.maximum(m, sinks[hh])
        p = jnp.exp(s - m)
        den = jnp.sum(p, axis=-1, keepdims=True)
        if sinks is not None:
            den = den + jnp.exp(sinks[hh] - m)
        o = jnp.dot(p.astype(BF16), vcat, preferred_element_type=F32)
        outs.append(o / den)
        lses.append(m + jnp.log(den))
    return jnp.where(first, outs[0], outs[1]), jnp.where(first, lses[0], lses[1])


def _attn_a_kernel(q_ref, kc_ref, kp_ref, vc_ref, vp_ref, o_ref, lse_ref, *, max_dist):
    valid = _band_valid(pl.program_id(2), max_dist)
    for p in range(4):
        sl = slice(p * LANES, (p + 1) * LANES)
        kcat = jnp.concatenate([kp_ref[0, :, sl], kc_ref[0, :, sl]], axis=0)
        vcat = jnp.concatenate([vp_ref[0, :, sl], vc_ref[0, :, sl]], axis=0)
        o, lse = _attend_pair(q_ref[0, :, sl], kcat, vcat, valid, None)
        o_ref[0, :, sl] = o.astype(BF16)
        lse_ref[0, :, sl] = lse


def _attn_a(ua, bsz, seq, window, dil):
    l = seq // dil
    steps = window // dil
    view = ua.reshape(bsz, l, dil * UA_W)
    blk = (1, BLK, 512)
    cur = lambda c: (lambda b, r, j: (b, j, 3 * r + c))
    prev = lambda c: (lambda b, r, j: (b, jnp.maximum(j - 1, 0), 3 * r + c))
    out = lambda b, r, j: (b, j, r)
    o, lse = pl.pallas_call(
        functools.partial(_attn_a_kernel, max_dist=steps),
        grid=(bsz, dil, l // BLK),
        in_specs=[pl.BlockSpec(blk, cur(0)), pl.BlockSpec(blk, cur(1)), pl.BlockSpec(blk, prev(1)),
                  pl.BlockSpec(blk, cur(2)), pl.BlockSpec(blk, prev(2))],
        out_specs=[pl.BlockSpec(blk, out), pl.BlockSpec(blk, out)],
        out_shape=[jax.ShapeDtypeStruct((bsz, l, dil * 512), BF16),
                   jax.ShapeDtypeStruct((bsz, l, dil * 512), F32)],
        compiler_params=_cparams(("parallel", "parallel", "arbitrary")),
        name=f"attn_a_d{dil}",
    )(view, view, view, view, view)
    return o.reshape(bsz * seq, 512), lse.reshape(bsz * seq, 512)


def _attn_b_kernel(sink_ref, q_ref, kc_ref, kp_ref, vc_ref, vp_ref, o_ref):
    valid = _band_valid(pl.program_id(1), B_WINDOW - 1)
    lane = lax.broadcasted_iota(jnp.int32, (2 * BLK, LANES), 1)
    first = lane < HEAD_DIM
    k = jnp.concatenate([kp_ref[0], kc_ref[0]], axis=0)
    v = jnp.concatenate([vp_ref[0], vc_ref[0]], axis=0)
    k_sw = pltpu.roll(k, HEAD_DIM, 1)
    v_sw = pltpu.roll(v, HEAD_DIM, 1)
    kdup = (jnp.where(first, k, k_sw), jnp.where(first, k_sw, k))
    vdup = (jnp.where(first, v, v_sw), jnp.where(first, v_sw, v))
    for p in range(4):
        sl = slice(p * LANES, (p + 1) * LANES)
        kv = p // 2
        sinks = (sink_ref[2 * p], sink_ref[2 * p + 1])
        o, _ = _attend_pair(q_ref[0, :, sl], kdup[kv], vdup[kv], valid, sinks)
        o_ref[0, :, sl] = o.astype(BF16)


def _attn_b(ub, sinks, bsz, seq):
    view = ub.reshape(bsz, seq, UB_W)
    kvblk = (1, BLK, LANES)
    cur = lambda c: (lambda b, j: (b, j, c))
    prev = lambda c: (lambda b, j: (b, jnp.maximum(j - 1, 0), c))
    o = pl.pallas_call(
        _attn_b_kernel,
        grid=(bsz, seq // BLK),
        in_specs=[pl.BlockSpec(memory_space=pltpu.SMEM),
                  pl.BlockSpec((1, BLK, 512), lambda b, j: (b, j, 0)),
                  pl.BlockSpec(kvblk, cur(4)), pl.BlockSpec(kvblk, prev(4)),
                  pl.BlockSpec(kvblk, cur(5)), pl.BlockSpec(kvblk, prev(5))],
        out_specs=pl.BlockSpec((1, BLK, 512), lambda b, j: (b, j, 0)),
        out_shape=jax.ShapeDtypeStruct((bsz, seq, 512), BF16),
        compiler_params=_cparams(("parallel", "arbitrary")),
        name="attn_b",
    )(sinks, view, view, view, view, view)
    return o.reshape(bsz * seq, 512)


def _split3(a):
    hi = a.astype(BF16)
    r = a - hi.astype(F32)
    mid = r.astype(BF16)
    lo = (r - mid.astype(F32)).astype(BF16)
    return hi, mid, lo


def _dot_nt(a, b):
    return lax.dot_general(a, b, (((1,), (1,)), ((), ())), preferred_element_type=F32)


def _dot_tn(a, b):
    return lax.dot_general(a, b, (((0,), (0,)), ((), ())), preferred_element_type=F32)


def _mm(a, b):
    return jnp.dot(a.astype(BF16), b.astype(BF16), preferred_element_type=F32)


GDN_TT = 512
GDN_NCH = GDN_TT // C_CHUNK
GDN_CPI = 2


def _gdn_kernel(cq_ref, ck_ref, cv_ref, zs_ref, sm_ref, cw_ref, alog_ref, dtb_ref, cn_ref, e_ref, lbd_ref,
                y_ref, state_ref, ext_ref, expd_ref, u_ref, w_ref, qd_ref, kd_ref, attn_ref):
    c, tt = C_CHUNK, GDN_TT

    @pl.when(pl.program_id(1) == 0)
    def _():
        state_ref[...] = jnp.zeros_like(state_ref)
        ext_ref[0:8, :] = jnp.zeros((8, 2048), F32)

    ext_ref[8:8 + tt, 0:512] = cq_ref[...].astype(F32)
    ext_ref[8:8 + tt, 512:1024] = ck_ref[...].astype(F32)
    ext_ref[8:8 + tt, 1024:2048] = cv_ref[...].astype(F32)

    sm = sm_ref[...]
    lane = lax.broadcasted_iota(jnp.int32, (tt, LANES), 1)
    z = sm + dtb_ref[...]
    softplus = jnp.maximum(z, 0.0) + jnp.log1p(jnp.exp(-jnp.abs(z)))
    g = -jnp.exp(alog_ref[...]) * softplus
    beta = jax.nn.sigmoid(sm)
    lbd = lbd_ref[...]
    g_hi, g_mid, g_lo = _split3(g)
    gc = (jnp.dot(lbd, g_hi, preferred_element_type=F32) + jnp.dot(lbd, g_mid, preferred_element_type=F32)
          + jnp.dot(lbd, g_lo, preferred_element_type=F32))
    c_hi, c_mid, c_lo = _split3(jnp.where(lane < C_V_HEADS, gc, beta))
    for s in range(4):
        sl = slice(s * 512, (s + 1) * 512)
        expd_ref[:, sl] = (jnp.dot(c_hi, e_ref[:, sl], preferred_element_type=F32)
                           + jnp.dot(c_mid, e_ref[:, sl], preferred_element_type=F32)
                           + jnp.dot(c_lo, e_ref[:, sl], preferred_element_type=F32))

    ri = lax.broadcasted_iota(jnp.int32, (c, c), 0)
    ci = lax.broadcasted_iota(jnp.int32, (c, c), 1)
    tril = ri >= ci
    strict = ri > ci
    eye = (ri == ci).astype(F32)

    def l2n(t):
        return t * lax.rsqrt(jnp.sum(t * t, axis=-1, keepdims=True) + EPS)

    def conv(r0, col0, width):
        xs = ext_ref[pl.ds(r0, c + 8), col0:col0 + width]
        y = xs[5:5 + c] * cw_ref[0:1, col0:col0 + width]
        for j in range(1, C_CONV):
            y = y + xs[5 + j:5 + j + c] * cw_ref[j:j + 1, col0:col0 + width]
        return y * jax.nn.sigmoid(y)

    def chunk_local(i, carry):
        units = []
        for j in range(GDN_CPI):
            r0 = pl.multiple_of((i * GDN_CPI + j) * c, c)
            rows = pl.ds(r0, c)
            kqs = []
            for p in range(C_V_HEADS // 2):
                qn = l2n(conv(r0, p * C_DK, C_DK)) * (C_DK ** -0.5)
                kn = l2n(conv(r0, 512 + p * C_DK, C_DK))
                kqs.append((qn, kn))
            for h in range(C_V_HEADS):
                units.append(dict(rows=rows, h=h, qn=kqs[h // 2][0], kn=kqs[h // 2][1], p=(j, h // 2),
                                  v=conv(r0, 1024 + h * C_DV, C_DV),
                                  ge=expd_ref[rows, h * LANES:(h + 1) * LANES],
                                  be=expd_ref[rows, 1024 + h * LANES:1024 + (h + 1) * LANES]))
        kq = {}
        for un in units:
            if un["p"] not in kq:
                k16 = un["kn"].astype(BF16)
                kq[un["p"]] = _dot_nt(jnp.concatenate([k16, un["qn"].astype(BF16)], axis=0), k16)
        for un in units:
            ge, be, kqh = un["ge"], un["be"], kq[un["p"]]
            diff = ge[:, 0:c] - ge.T[0:c, :]
            dec = jnp.where(tril, jnp.exp(jnp.where(tril, diff, 0.0)), 0.0)
            un["pw"] = jnp.where(strict, kqh[0:c] * dec * be[:, 0:c], 0.0) * -1.0
            un["attn"] = jnp.where(tril, kqh[c:] * dec, 0.0).astype(BF16)
            un["x"] = eye + un["pw"]
        for un in units:
            un["pw"] = _mm(un["pw"], un["pw"])
        for _ in range(4):
            for un in units:
                r = _mm(jnp.concatenate([un["pw"], un["x"]], axis=0), un["pw"])
                un["x"] = un["x"] + r[c:]
                un["pw"] = r[0:c]
        for un in units:
            un["x"] = un["x"] + _mm(un["x"], un["pw"])
        for un in units:
            ge, be = un["ge"], un["be"]
            eg = jnp.exp(ge)
            un["uw"] = _mm(un["x"], jnp.concatenate([un["v"] * be, un["kn"] * be * eg], axis=1))
            un["qd"] = (un["qn"] * eg).astype(BF16)
            un["kd"] = (un["kn"] * jnp.exp(ge[c - 1:c, :] - ge)).astype(BF16)
        for un in units:
            rows, h = un["rows"], un["h"]
            hs = slice(h * C_DV, (h + 1) * C_DV)
            u_ref[rows, hs] = un["uw"][:, 0:C_DV]
            w_ref[rows, hs] = un["uw"][:, C_DV:].astype(BF16)
            qd_ref[rows, hs] = un["qd"]
            kd_ref[rows, hs] = un["kd"]
            attn_ref[rows, h * LANES:h * LANES + c] = un["attn"]
        return carry

    lax.fori_loop(0, GDN_NCH // GDN_CPI, chunk_local, 0)
    ext_ref[0:8, :] = ext_ref[tt:tt + 8, :]
    cn = cn_ref[...]

    def recur(i, carry):
        r0 = pl.multiple_of(i * c, c)
        rows = pl.ds(r0, c)
        last8 = pl.ds(pl.multiple_of(r0 + c - 8, 8), 8)
        hsl = [slice(h * C_DV, (h + 1) * C_DV) for h in range(C_V_HEADS)]
        heads = range(C_V_HEADS)
        s = [state_ref[h] for h in heads]
        wq = [jnp.dot(jnp.concatenate([w_ref[rows, hsl[h]], qd_ref[rows, hsl[h]]], axis=0), s[h].astype(BF16),
                      preferred_element_type=F32) for h in heads]
        v16 = [(u_ref[rows, hsl[h]] - wq[h][0:c]).astype(BF16) for h in heads]
        s_new = [s[h] * jnp.exp(expd_ref[last8, hsl[h]][7:8]) + _dot_tn(kd_ref[rows, hsl[h]], v16[h]) for h in heads]
        o = [wq[h][c:] + jnp.dot(attn_ref[rows, h * LANES:h * LANES + c], v16[h], preferred_element_type=F32)
             for h in heads]
        for h in heads:
            state_ref[h] = s_new[h]
            y_ref[rows, hsl[h]] = (_rms(o[h], cn) * zs_ref[rows, hsl[h]].astype(F32)).astype(BF16)
        return carry

    lax.fori_loop(0, GDN_NCH, recur, 0)


def _gdn(uc, sm, conv_w, a_log, dt_bias, c_norm, bsz, seq):
    c, tt = C_CHUNK, GDN_TT
    nt = seq // tt
    pad = lambda v: jnp.pad(v.astype(F32), (0, LANES - v.shape[0])).reshape(1, LANES)
    li = jnp.arange(LANES)[:, None]
    cj = jnp.arange(2048)[None, :]
    e = (((li < 8) & (cj < 1024) & (cj // LANES == li))
         | ((li >= 8) & (li < 16) & (cj >= 1024) & ((cj - 1024) // LANES == li - 8))).astype(BF16)
    ti = jnp.arange(tt)
    lbd = ((ti[:, None] // c == ti[None, :] // c) & (ti[:, None] >= ti[None, :])).astype(BF16)
    row = lambda blockcol: (lambda b, t: (b * nt + t, blockcol))
    const = lambda b, t: (0, 0)
    return pl.pallas_call(
        _gdn_kernel,
        grid=(bsz, nt),
        in_specs=[
            pl.BlockSpec((tt, 512), row(0)),
            pl.BlockSpec((tt, 512), row(1)),
            pl.BlockSpec((tt, 1024), row(1)),
            pl.BlockSpec((tt, 1024), row(2)),
            pl.BlockSpec((tt, SM_W), row(0)),
            pl.BlockSpec((C_CONV, 2048), const),
            pl.BlockSpec((1, LANES), const),
            pl.BlockSpec((1, LANES), const),
            pl.BlockSpec((1, LANES), const),
            pl.BlockSpec((LANES, 2048), const),
            pl.BlockSpec((tt, tt), const),
        ],
        out_specs=pl.BlockSpec((tt, 1024), row(0)),
        out_shape=jax.ShapeDtypeStruct((bsz * seq, 1024), BF16),
        scratch_shapes=[
            pltpu.VMEM((C_V_HEADS, C_DK, C_DV), F32),
            pltpu.VMEM((tt + 8, 2048), F32),
            pltpu.VMEM((tt, 2048), F32),
            pltpu.VMEM((tt, 1024), F32),
            pltpu.VMEM((tt, 1024), BF16),
            pltpu.VMEM((tt, 1024), BF16),
            pltpu.VMEM((tt, 1024), BF16),
            pltpu.VMEM((tt, 1024), BF16),
        ],
        compiler_params=_cparams(("parallel", "arbitrary")),
        name="gdn",
    )(uc, uc, uc, uc, sm, conv_w.astype(F32), pad(a_log), pad(dt_bias), c_norm.astype(F32).reshape(1, LANES), e, lbd)


def _merge_kernel(x_ref, o1_ref, o2_ref, o3_ref, l1_ref, l2_ref, l3_ref, yb_ref, yc_ref, ug_ref,
                  wa_ref, wb_ref, wc_ref, wo_ref, out_ref):
    l1, l2, l3 = l1_ref[...], l2_ref[...], l3_ref[...]
    m = jnp.maximum(jnp.maximum(l1, l2), l3)
    e1, e2, e3 = jnp.exp(l1 - m), jnp.exp(l2 - m), jnp.exp(l3 - m)
    ya = (e1 * o1_ref[...].astype(F32) + e2 * o2_ref[...].astype(F32) + e3 * o3_ref[...].astype(F32)) / (e1 + e2 + e3)
    merged = (ug_ref[:, 0:1024].astype(F32) * jnp.dot(ya.astype(BF16), wa_ref[...], preferred_element_type=F32)
              + ug_ref[:, 1024:2048].astype(F32) * jnp.dot(yb_ref[...], wb_ref[...], preferred_element_type=F32)
              + ug_ref[:, 2048:3072].astype(F32) * jnp.dot(yc_ref[...], wc_ref[...], preferred_element_type=F32))
    out_ref[...] = x_ref[...] + jnp.dot(merged.astype(BF16), wo_ref[...], preferred_element_type=F32)


def _merge(x2, o_list, lse_list, yb, yc, ug, wa, wb, wc, wo, tm):
    m = x2.shape[0]
    row = lambda i: (i, 0)
    const = lambda i: (0, 0)
    rs = lambda w: pl.BlockSpec((tm, w), row)
    return pl.pallas_call(
        _merge_kernel,
        grid=(m // tm,),
        in_specs=[rs(D_MODEL), rs(512), rs(512), rs(512), rs(512), rs(512), rs(512), rs(512), rs(1024), rs(UG_W),
                  pl.BlockSpec((512, D_MODEL), const), pl.BlockSpec((512, D_MODEL), const),
                  pl.BlockSpec((1024, D_MODEL), const), pl.BlockSpec((D_MODEL, D_MODEL), const)],
        out_specs=rs(D_MODEL),
        out_shape=jax.ShapeDtypeStruct((m, D_MODEL), F32),
        compiler_params=_cparams(("parallel",)),
        name="merge",
    )(x2, *o_list, *lse_list, yb, yc, ug, wa, wb, wc, wo)


def _ffn_kernel(x_ref, g_ref, w1_ref, w2_ref, gf_ref, out_ref, *, final_norm):
    x = x_ref[...]
    h = _rms(x, g_ref[...]).astype(BF16)
    acc = x
    for c in range(D_FF // 1024):
        sl = slice(c * 1024, (c + 1) * 1024)
        a = jnp.maximum(jnp.dot(h, w1_ref[:, sl], preferred_element_type=F32), 0.0)
        acc = acc + jnp.dot((a * a).astype(BF16), w2_ref[sl, :], preferred_element_type=F32)
    out_ref[...] = _rms(acc, gf_ref[...]) if final_norm else acc


def _ffn(x2, gain, w1, w2, gain_final, final_norm, tm):
    m = x2.shape[0]
    row = lambda i: (i, 0)
    const = lambda i: (0, 0)
    return pl.pallas_call(
        functools.partial(_ffn_kernel, final_norm=final_norm),
        grid=(m // tm,),
        in_specs=[pl.BlockSpec((tm, D_MODEL), row), pl.BlockSpec((1, D_MODEL), const),
                  pl.BlockSpec((D_MODEL, D_FF), const), pl.BlockSpec((D_FF, D_MODEL), const),
                  pl.BlockSpec((1, D_MODEL), const)],
        out_specs=pl.BlockSpec((tm, D_MODEL), row),
        out_shape=jax.ShapeDtypeStruct((m, D_MODEL), F32),
        compiler_params=_cparams(("parallel",)),
        name="ffn",
    )(x2, gain, w1, w2, gain_final)


def _rope_tables(positions):
    half = ROT_DIM // 2
    inv_freq = jnp.power(ROPE_THETA, -jnp.arange(0, ROT_DIM, 2, dtype=F32) / ROT_DIM)
    ang = positions.astype(F32).reshape(-1, 1) * inv_freq
    cos, sin = jnp.cos(ang), jnp.sin(ang)
    n = ang.shape[0]
    rest = HEAD_DIM - ROT_DIM
    cos64 = jnp.concatenate([cos, cos, jnp.ones((n, rest), F32)], axis=1)
    s1 = jnp.concatenate([-sin, jnp.zeros((n, half + rest), F32)], axis=1)
    s2 = jnp.concatenate([jnp.zeros((n, half), F32), sin, jnp.zeros((n, rest), F32)], axis=1)
    two = lambda t: jnp.concatenate([t, t], axis=1)
    return two(cos64), two(s1), two(s2)


def kernel(x, positions, norm_mix, w_in, b_in, conv_w, a_log, dt_bias, sinks, c_norm, w_branch_a, w_branch_b,
           w_branch_c, w_out, norm_ffn, w_ff1, w_ff2, norm_final):
    bsz, seq, d = x.shape
    depth = w_in.shape[0]
    assert d == D_MODEL and seq % (16 * BLK) == 0
    m = bsz * seq
    cosv, s1, s2 = _rope_tables(positions)
    x2 = x.reshape(m, d)
    ca = 5376
    for layer in range(depth):
        wl, bl = w_in[layer], b_in[layer]
        w_all = jnp.concatenate([wl[:, :ca], wl[:, ca + 16:], wl[:, ca:ca + 16],
                                 jnp.zeros((d, SM_W - 16), F32)], axis=1).astype(BF16)
        b_all = jnp.concatenate([bl[:ca], bl[ca + 16:], bl[ca:ca + 16], jnp.zeros((SM_W - 16,), F32)]).reshape(1, -1)
        ua, ub, uc, ug, sm = _inproj(x2, norm_mix[layer].reshape(1, d), w_all, b_all, cosv, s1, s2, tm=256)
        o_list, lse_list = [], []
        for window, dil in A_CONFIGS:
            o, lse = _attn_a(ua, bsz, seq, window, dil)
            o_list.append(o)
            lse_list.append(lse)
        yb = _attn_b(ub, sinks[layer].astype(F32), bsz, seq)
        yc = _gdn(uc, sm, conv_w[layer], a_log[layer], dt_bias[layer], c_norm[layer], bsz, seq)
        x2 = _merge(x2, o_list, lse_list, yb, yc, ug, w_branch_a[layer].astype(BF16), w_branch_b[layer].astype(BF16),
                    w_branch_c[layer].astype(BF16), w_out[layer].astype(BF16), tm=512)
        x2 = _ffn(x2, norm_ffn[layer].reshape(1, d), w_ff1[layer].astype(BF16), w_ff2[layer].astype(BF16),
                  norm_final.reshape(1, d), final_norm=(layer == depth - 1), tm=512)
    return x2.reshape(bsz, seq, d)
```

```python
import functools
import math

import jax
import jax.numpy as jnp
from jax import lax
from jax.experimental import pallas as pl
from jax.experimental.pallas import tpu as pltpu

F32 = jnp.float32
BF16 = jnp.bfloat16

D_MODEL = 1024
HEAD_DIM = 64
ROT_DIM = 16
ROPE_THETA = 500000.0
BLK = 128
NEG_INF = -1e30
EPS = 1e-6
A_CONFIGS = ((128, 1), (512, 4), (2048, 16))
B_WINDOW = 128
C_V_HEADS = 8
C_DK = 128
C_DV = 128
C_CONV = 4
C_CHUNK = 64
D_FF = 4096

LANES = 128
UA_W = 1536
UB_W = 768
UC_W = 3072
UG_W = 3072
SM_W = 128
W_ALL = UA_W + UB_W + UC_W + UG_W + SM_W
VMEM_LIMIT = 56 * 1024 * 1024


def _cparams(sem):
    return pltpu.CompilerParams(dimension_semantics=sem, vmem_limit_bytes=VMEM_LIMIT)


def _rms(x, gain):
    return x * lax.rsqrt(jnp.mean(x * x, axis=-1, keepdims=True) + EPS) * gain


def _inproj_kernel(x_ref, g_ref, w_ref, b_ref, cos_ref, s1_ref, s2_ref,
                   ua_ref, ua4_ref, ua16_ref, ub_ref, uc_ref, ug_ref, sm_ref, stage_ref):
    h = _rms(x_ref[...], g_ref[...]).astype(BF16)
    cosv, s1, s2 = cos_ref[...], s1_ref[...], s2_ref[...]

    def seg(off, width):
        return (jnp.dot(h, w_ref[:, off:off + width], preferred_element_type=F32)
                + b_ref[:, off:off + width])

    def rope(a, scale):
        parts = []
        for g in range(a.shape[1] // LANES):
            t = a[:, g * LANES:(g + 1) * LANES]
            r = t * cosv + pltpu.roll(t, LANES - 8, 1) * s1 + pltpu.roll(t, 8, 1) * s2
            parts.append(r * scale if scale != 1.0 else r)
        return jnp.concatenate(parts, axis=1)

    qscale = HEAD_DIM ** -0.5
    tm = x_ref.shape[0]
    for c, val in enumerate((rope(seg(0, 512), qscale), rope(seg(512, 512), 1.0), seg(1024, 512))):
        ua_ref[:, c * 512:(c + 1) * 512] = val.astype(BF16)
        for gl in range(4):
            g = 4 * c + gl
            stage_ref[g] = val[:, gl * LANES:(gl + 1) * LANES]
            for dil, dst in ((4, ua4_ref), (16, ua16_ref)):
                for r in range(dil):
                    col = r * UA_W + g * LANES
                    dst[:, col:col + LANES] = stage_ref[g, pl.ds(r, tm // dil, stride=dil), :].astype(BF16)
    ub_ref[:, 0:512] = rope(seg(1536, 512), qscale).astype(BF16)
    ub_ref[:, 512:640] = rope(seg(2048, 128), 1.0).astype(BF16)
    ub_ref[:, 640:768] = seg(2176, 128).astype(BF16)
    off = UA_W + UB_W
    for c in range(4):
        uc_ref[:, c * 512:(c + 1) * 512] = seg(off + c * 512, 512).astype(BF16)
    for c in range(4, 6):
        z = seg(off + c * 512, 512)
        uc_ref[:, c * 512:(c + 1) * 512] = (z * jax.nn.sigmoid(z)).astype(BF16)
    off += UC_W
    for c in range(6):
        ug_ref[:, c * 512:(c + 1) * 512] = jax.nn.sigmoid(seg(off + c * 512, 512)).astype(BF16)
    off += UG_W
    sm_ref[...] = seg(off, SM_W)


def _inproj(x2, gain, w_all, b_all, cosv, s1, s2, tm):
    m = x2.shape[0]
    row = lambda i: (i, 0)
    const = lambda i: (0, 0)
    return pl.pallas_call(
        _inproj_kernel,
        grid=(m // tm,),
        in_specs=[
            pl.BlockSpec((tm, D_MODEL), row),
            pl.BlockSpec((1, D_MODEL), const),
            pl.BlockSpec((D_MODEL, W_ALL), const),
            pl.BlockSpec((1, W_ALL), const),
            pl.BlockSpec((tm, LANES), row),
            pl.BlockSpec((tm, LANES), row),
            pl.BlockSpec((tm, LANES), row),
        ],
        out_specs=[
            pl.BlockSpec((tm, UA_W), row),
            pl.BlockSpec((tm // 4, 4 * UA_W), row),
            pl.BlockSpec((tm // 16, 16 * UA_W), row),
            pl.BlockSpec((tm, UB_W), row),
            pl.BlockSpec((tm, UC_W), row),
            pl.BlockSpec((tm, UG_W), row),
            pl.BlockSpec((tm, SM_W), row),
        ],
        out_shape=[
            jax.ShapeDtypeStruct((m, UA_W), BF16),
            jax.ShapeDtypeStruct((m // 4, 4 * UA_W), BF16),
            jax.ShapeDtypeStruct((m // 16, 16 * UA_W), BF16),
            jax.ShapeDtypeStruct((m, UB_W), BF16),
            jax.ShapeDtypeStruct((m, UC_W), BF16),
            jax.ShapeDtypeStruct((m, UG_W), BF16),
            jax.ShapeDtypeStruct((m, SM_W), F32),
        ],
        scratch_shapes=[pltpu.VMEM((UA_W // LANES, tm, LANES), F32)],
        compiler_params=_cparams(("parallel",)),
        name="inproj",
    )(x2, gain, w_all, b_all, cosv, s1, s2)


ATT_QT = 512


def _band_valid(max_dist, first_block):
    qi = lax.broadcasted_iota(jnp.int32, (BLK, 2 * BLK), 0)
    kj = lax.broadcasted_iota(jnp.int32, (BLK, 2 * BLK), 1)
    dist = BLK + qi - kj
    band = (dist >= 0) & (dist <= max_dist)
    if first_block is None:
        return band
    return band & (jnp.logical_not(first_block) | (kj >= BLK))


def _kv_window(cur_ref, prev_ref, i, sl):
    if i == 0:
        return jnp.concatenate([prev_ref[0, :, sl], cur_ref[0, 0:BLK, sl]], axis=0)
    return cur_ref[0, (i - 1) * BLK:(i + 1) * BLK, sl]


def _attend_pair(qp, kcat, vcat, valid, sinks):
    lane = lax.broadcasted_iota(jnp.int32, (BLK, LANES), 1)
    first = lane < HEAD_DIM
    outs, lses = [], []
    for hh in range(2):
        qm = jnp.where(first if hh == 0 else ~first, qp, jnp.zeros_like(qp))
        s = lax.dot_general(qm, kcat, (((1,), (1,)), ((), ())), preferred_element_type=F32)
        s = jnp.where(valid, s, NEG_INF)
        m = jnp.max(s, axis=-1, keepdims=True)
        if sinks is not None:
            m = jnp.maximum(m, sinks[hh])
        p = jnp.exp(s - m)
        den = jnp.sum(p, axis=-1, keepdims=True)
        if sinks is not None:
            den = den + jnp.exp(sinks[hh] - m)
        o = jnp.dot(p.astype(BF16), vcat, preferred_element_type=F32)
        outs.append(o / den)
        lses.append(m + jnp.log(den))
    return jnp.where(first, outs[0], outs[1]), jnp.where(first, lses[0], lses[1])


def _attn_a_kernel(q_ref, kc_ref, kp_ref, vc_ref, vp_ref, o_ref, lse_ref, *, max_dist):
    first_tile = pl.program_id(2) == 0
    for i in range(q_ref.shape[1] // BLK):
        valid = _band_valid(max_dist, first_tile if i == 0 else None)
        rows = slice(i * BLK, (i + 1) * BLK)
        for p in range(4):
            sl = slice(p * LANES, (p + 1) * LANES)
            o, lse = _attend_pair(q_ref[0, rows, sl], _kv_window(kc_ref, kp_ref, i, sl),
                                  _kv_window(vc_ref, vp_ref, i, sl), valid, None)
            o_ref[0, rows, sl] = o.astype(BF16)
            lse_ref[0, rows, sl] = lse


def _attn_a(ua_d, bsz, seq, window, dil):
    l = seq // dil
    steps = window // dil
    view = ua_d.reshape(bsz, l, dil * UA_W)
    qt = min(ATT_QT, l)
    blk = (1, qt, 512)
    pblk = (1, BLK, 512)
    cur = lambda c: (lambda b, r, j: (b, j, 3 * r + c))
    prev = lambda c: (lambda b, r, j: (b, jnp.maximum(j * (qt // BLK) - 1, 0), 3 * r + c))
    out = lambda b, r, j: (b, j, r)
    o, lse = pl.pallas_call(
        functools.partial(_attn_a_kernel, max_dist=steps),
        grid=(bsz, dil, l // qt),
        in_specs=[pl.BlockSpec(blk, cur(0)), pl.BlockSpec(blk, cur(1)), pl.BlockSpec(pblk, prev(1)),
                  pl.BlockSpec(blk, cur(2)), pl.BlockSpec(pblk, prev(2))],
        out_specs=[pl.BlockSpec(blk, out), pl.BlockSpec(blk, out)],
        out_shape=[jax.ShapeDtypeStruct((bsz, l, dil * 512), BF16),
                   jax.ShapeDtypeStruct((bsz, l, dil * 512), F32)],
        compiler_params=_cparams(("parallel", "parallel", "arbitrary")),
        name=f"attn_a_d{dil}",
    )(view, view, view, view, view)
    return o.reshape(bsz * l, dil * 512), lse.reshape(bsz * l, dil * 512)


def _attn_b_kernel(sink_ref, q_ref, kc_ref, kp_ref, vc_ref, vp_ref, o_ref):
    first_tile = pl.program_id(1) == 0
    lane = lax.broadcasted_iota(jnp.int32, (2 * BLK, LANES), 1)
    first = lane < HEAD_DIM
    for i in range(q_ref.shape[1] // BLK):
        valid = _band_valid(B_WINDOW - 1, first_tile if i == 0 else None)
        rows = slice(i * BLK, (i + 1) * BLK)
        k = _kv_window(kc_ref, kp_ref, i, slice(0, LANES))
        v = _kv_window(vc_ref, vp_ref, i, slice(0, LANES))
        k_sw = pltpu.roll(k, HEAD_DIM, 1)
        v_sw = pltpu.roll(v, HEAD_DIM, 1)
        kdup = (jnp.where(first, k, k_sw), jnp.where(first, k_sw, k))
        vdup = (jnp.where(first, v, v_sw), jnp.where(first, v_sw, v))
        for p in range(4):
            sl = slice(p * LANES, (p + 1) * LANES)
            kv = p // 2
            sinks = (sink_ref[2 * p], sink_ref[2 * p + 1])
            o, _ = _attend_pair(q_ref[0, rows, sl], kdup[kv], vdup[kv], valid, sinks)
            o_ref[0, rows, sl] = o.astype(BF16)


def _attn_b(ub, sinks, bsz, seq):
    view = ub.reshape(bsz, seq, UB_W)
    qt = ATT_QT
    cur = lambda c: (lambda b, j: (b, j, c))
    prev = lambda c: (lambda b, j: (b, jnp.maximum(j * (qt // BLK) - 1, 0), c))
    o = pl.pallas_call(
        _attn_b_kernel,
        grid=(bsz, seq // qt),
        in_specs=[pl.BlockSpec(memory_space=pltpu.SMEM),
                  pl.BlockSpec((1, qt, 512), lambda b, j: (b, j, 0)),
                  pl.BlockSpec((1, qt, LANES), cur(4)), pl.BlockSpec((1, BLK, LANES), prev(4)),
                  pl.BlockSpec((1, qt, LANES), cur(5)), pl.BlockSpec((1, BLK, LANES), prev(5))],
        out_specs=pl.BlockSpec((1, qt, 512), lambda b, j: (b, j, 0)),
        out_shape=jax.ShapeDtypeStruct((bsz, seq, 512), BF16),
        compiler_params=_cparams(("parallel", "arbitrary")),
        name="attn_b",
    )(sinks, view, view, view, view, view)
    return o.reshape(bsz * seq, 512)


def _split3(a):
    hi = a.astype(BF16)
    r = a - hi.astype(F32)
    mid = r.astype(BF16)
    lo = (r - mid.astype(F32)).astype(BF16)
    return hi, mid, lo


def _dot_nt(a, b):
    return lax.dot_general(a, b, (((1,), (1,)), ((), ())), preferred_element_type=F32)


def _dot_tn(a, b):
    return lax.dot_general(a, b, (((0,), (0,)), ((), ())), preferred_element_type=F32)


def _mm(a, b):
    return jnp.dot(a.astype(BF16), b.astype(BF16), preferred_element_type=F32)


GDN_TT = 512
GDN_NCH = GDN_TT // C_CHUNK
GDN_CPI = 2


def _gdn_kernel(cq_ref, ck_ref, cv_ref, zs_ref, sm_ref, cw_ref, alog_ref, dtb_ref, cn_ref, e_ref, lbd_ref,
                y_ref, state_ref, ext_ref, expd_ref, u_ref, w_ref, qd_ref, kd_ref, attn_ref):
    c, tt = C_CHUNK, GDN_TT

    @pl.when(pl.program_id(1) == 0)
    def _():
        state_ref[...] = jnp.zeros_like(state_ref)
        ext_ref[0:8, :] = jnp.zeros((8, 2048), F32)

    ext_ref[8:8 + tt, 0:512] = cq_ref[...].astype(F32)
    ext_ref[8:8 + tt, 512:1024] = ck_ref[...].astype(F32)
    ext_ref[8:8 + tt, 1024:2048] = cv_ref[...].astype(F32)

    sm = sm_ref[...]
    lane = lax.broadcasted_iota(jnp.int32, (tt, LANES), 1)
    z = sm + dtb_ref[...]
    softplus = jnp.maximum(z, 0.0) + jnp.log1p(jnp.exp(-jnp.abs(z)))
    g = -jnp.exp(alog_ref[...]) * softplus
    beta = jax.nn.sigmoid(sm)
    lbd = lbd_ref[...]
    g_hi, g_mid, g_lo = _split3(g)
    gc = (jnp.dot(lbd, g_hi, preferred_element_type=F32) + jnp.dot(lbd, g_mid, preferred_element_type=F32)
          + jnp.dot(lbd, g_lo, preferred_element_type=F32))
    c_hi, c_mid, c_lo = _split3(jnp.where(lane < C_V_HEADS, gc, beta))
    for s in range(4):
        sl = slice(s * 512, (s + 1) * 512)
        expd_ref[:, sl] = (jnp.dot(c_hi, e_ref[:, sl], preferred_element_type=F32)
                           + jnp.dot(c_mid, e_ref[:, sl], preferred_element_type=F32)
                           + jnp.dot(c_lo, e_ref[:, sl], preferred_element_type=F32))

    ri = lax.broadcasted_iota(jnp.int32, (c, c), 0)
    ci = lax.broadcasted_iota(jnp.int32, (c, c), 1)
    tril = ri >= ci
    strict = ri > ci
    eye = (ri == ci).astype(F32)

    def l2n(t):
        return t * lax.rsqrt(jnp.sum(t * t, axis=-1, keepdims=True) + EPS)

    def conv(r0, col0, width):
        xs = ext_ref[pl.ds(r0, c + 8), col0:col0 + width]
        y = xs[5:5 + c] * cw_ref[0:1, col0:col0 + width]
        for j in range(1, C_CONV):
            y = y + xs[5 + j:5 + j + c] * cw_ref[j:j + 1, col0:col0 + width]
        return y * jax.nn.sigmoid(y)

    def chunk_local(i, carry):
        units = []
        for j in range(GDN_CPI):
            r0 = pl.multiple_of((i * GDN_CPI + j) * c, c)
            rows = pl.ds(r0, c)
            kqs = []
            for p in range(C_V_HEADS // 2):
                qn = l2n(conv(r0, p * C_DK, C_DK)) * (C_DK ** -0.5)
                kn = l2n(conv(r0, 512 + p * C_DK, C_DK))
                kqs.append((qn, kn))
            for h in range(C_V_HEADS):
                units.append(dict(rows=rows, h=h, qn=kqs[h // 2][0], kn=kqs[h // 2][1], p=(j, h // 2),
                                  v=conv(r0, 1024 + h * C_DV, C_DV),
                                  ge=expd_ref[rows, h * LANES:(h + 1) * LANES],
                                  be=expd_ref[rows, 1024 + h * LANES:1024 + (h + 1) * LANES]))
        kq = {}
        for un in units:
            if un["p"] not in kq:
                k16 = un["kn"].astype(BF16)
                kq[un["p"]] = _dot_nt(jnp.concatenate([k16, un["qn"].astype(BF16)], axis=0), k16)
        for un in units:
            ge, be, kqh = un["ge"], un["be"], kq[un["p"]]
            diff = ge[:, 0:c] - ge.T[0:c, :]
            dec = jnp.where(tril, jnp.exp(jnp.where(tril, diff, 0.0)), 0.0)
            un["pw"] = jnp.where(strict, kqh[0:c] * dec * be[:, 0:c], 0.0) * -1.0
            un["attn"] = jnp.where(tril, kqh[c:] * dec, 0.0).astype(BF16)
            un["x"] = eye + un["pw"]
        for un in units:
            un["pw"] = _mm(un["pw"], un["pw"])
        for _ in range(4):
            for un in units:
                r = _mm(jnp.concatenate([un["pw"], un["x"]], axis=0), un["pw"])
                un["x"] = un["x"] + r[c:]
                un["pw"] = r[0:c]
        for un in units:
            un["x"] = un["x"] + _mm(un["x"], un["pw"])
        for un in units:
            ge, be = un["ge"], un["be"]
            eg = jnp.exp(ge)
            un["uw"] = _mm(un["x"], jnp.concatenate([un["v"] * be, un["kn"] * be * eg], axis=1))
            un["qd"] = (un["qn"] * eg).astype(BF16)
            un["kd"] = (un["kn"] * jnp.exp(ge[c - 1:c, :] - ge)).astype(BF16)
        for un in units:
            rows, h = un["rows"], un["h"]
            hs = slice(h * C_DV, (h + 1) * C_DV)
            u_ref[rows, hs] = un["uw"][:, 0:C_DV]
            w_ref[rows, hs] = un["uw"][:, C_DV:].astype(BF16)
            qd_ref[rows, hs] = un["qd"]
            kd_ref[rows, hs] = un["kd"]
            attn_ref[rows, h * LANES:h * LANES + c] = un["attn"]
        return carry

    lax.fori_loop(0, GDN_NCH // GDN_CPI, chunk_local, 0)
    ext_ref[0:8, :] = ext_ref[tt:tt + 8, :]
    cn = cn_ref[...]

    def recur(i, carry):
        r0 = pl.multiple_of(i * c, c)
        rows = pl.ds(r0, c)
        last8 = pl.ds(pl.multiple_of(r0 + c - 8, 8), 8)
        hsl = [slice(h * C_DV, (h + 1) * C_DV) for h in range(C_V_HEADS)]
        heads = range(C_V_HEADS)
        s = [state_ref[h] for h in heads]
        wq = [jnp.dot(jnp.concatenate([w_ref[rows, hsl[h]], qd_ref[rows, hsl[h]]], axis=0), s[h].astype(BF16),
                      preferred_element_type=F32) for h in heads]
        v16 = [(u_ref[rows, hsl[h]] - wq[h][0:c]).astype(BF16) for h in heads]
        s_new = [s[h] * jnp.exp(expd_ref[last8, hsl[h]][7:8]) + _dot_tn(kd_ref[rows, hsl[h]], v16[h]) for h in heads]
        o = [wq[h][c:] + jnp.dot(attn_ref[rows, h * LANES:h * LANES + c], v16[h], preferred_element_type=F32)
             for h in heads]
        for h in heads:
            state_ref[h] = s_new[h]
            y_ref[rows, hsl[h]] = (_rms(o[h], cn) * zs_ref[rows, hsl[h]].astype(F32)).astype(BF16)
        return carry

    lax.fori_loop(0, GDN_NCH, recur, 0)


def _gdn(uc, sm, conv_w, a_log, dt_bias, c_norm, bsz, seq):
    c, tt = C_CHUNK, GDN_TT
    nt = seq // tt
    pad = lambda v: jnp.pad(v.astype(F32), (0, LANES - v.shape[0])).reshape(1, LANES)
    li = jnp.arange(LANES)[:, None]
    cj = jnp.arange(2048)[None, :]
    e = (((li < 8) & (cj < 1024) & (cj // LANES == li))
         | ((li >= 8) & (li < 16) & (cj >= 1024) & ((cj - 1024) // LANES == li - 8))).astype(BF16)
    ti = jnp.arange(tt)
    lbd = ((ti[:, None] // c == ti[None, :] // c) & (ti[:, None] >= ti[None, :])).astype(BF16)
    row = lambda blockcol: (lambda b, t: (b * nt + t, blockcol))
    const = lambda b, t: (0, 0)
    return pl.pallas_call(
        _gdn_kernel,
        grid=(bsz, nt),
        in_specs=[
            pl.BlockSpec((tt, 512), row(0)),
            pl.BlockSpec((tt, 512), row(1)),
            pl.BlockSpec((tt, 1024), row(1)),
            pl.BlockSpec((tt, 1024), row(2)),
            pl.BlockSpec((tt, SM_W), row(0)),
            pl.BlockSpec((C_CONV, 2048), const),
            pl.BlockSpec((1, LANES), const),
            pl.BlockSpec((1, LANES), const),
            pl.BlockSpec((1, LANES), const),
            pl.BlockSpec((LANES, 2048), const),
            pl.BlockSpec((tt, tt), const),
        ],
        out_specs=pl.BlockSpec((tt, 1024), row(0)),
        out_shape=jax.ShapeDtypeStruct((bsz * seq, 1024), BF16),
        scratch_shapes=[
            pltpu.VMEM((C_V_HEADS, C_DK, C_DV), F32),
            pltpu.VMEM((tt + 8, 2048), F32),
            pltpu.VMEM((tt, 2048), F32),
            pltpu.VMEM((tt, 1024), F32),
            pltpu.VMEM((tt, 1024), BF16),
            pltpu.VMEM((tt, 1024), BF16),
            pltpu.VMEM((tt, 1024), BF16),
            pltpu.VMEM((tt, 1024), BF16),
        ],
        compiler_params=_cparams(("parallel", "arbitrary")),
        name="gdn",
    )(uc, uc, uc, uc, sm, conv_w.astype(F32), pad(a_log), pad(dt_bias), c_norm.astype(F32).reshape(1, LANES), e, lbd)


def _merge_kernel(x_ref, o1_ref, o2_ref, o3_ref, l1_ref, l2_ref, l3_ref, yb_ref, yc_ref, ug_ref,
                  wa_ref, wb_ref, wc_ref, wo_ref, out_ref, nat_ref):
    tm = x_ref.shape[0]
    for slot, (dil, src) in enumerate(((4, o2_ref), (4, l2_ref), (16, o3_ref), (16, l3_ref))):
        for r in range(dil):
            for g in range(4):
                col = r * 512 + g * LANES
                nat_ref[slot, g, pl.ds(r, tm // dil, stride=dil), :] = src[:, col:col + LANES].astype(F32)
    parts = []
    for g in range(4):
        sl = slice(g * LANES, (g + 1) * LANES)
        l1, l2, l3 = l1_ref[:, sl], nat_ref[1, g], nat_ref[3, g]
        m = jnp.maximum(jnp.maximum(l1, l2), l3)
        e1, e2, e3 = jnp.exp(l1 - m), jnp.exp(l2 - m), jnp.exp(l3 - m)
        parts.append(((e1 * o1_ref[:, sl].astype(F32) + e2 * nat_ref[0, g] + e3 * nat_ref[2, g])
                      / (e1 + e2 + e3)).astype(BF16))
    ya = jnp.concatenate(parts, axis=1)
    merged = (ug_ref[:, 0:1024].astype(F32) * jnp.dot(ya, wa_ref[...], preferred_element_type=F32)
              + ug_ref[:, 1024:2048].astype(F32) * jnp.dot(yb_ref[...], wb_ref[...], preferred_element_type=F32)
              + ug_ref[:, 2048:3072].astype(F32) * jnp.dot(yc_ref[...], wc_ref[...], preferred_element_type=F32))
    out_ref[...] = x_ref[...] + jnp.dot(merged.astype(BF16), wo_ref[...], preferred_element_type=F32)


def _merge(x2, o_list, lse_list, yb, yc, ug, wa, wb, wc, wo, tm):
    m = x2.shape[0]
    row = lambda i: (i, 0)
    const = lambda i: (0, 0)
    rs = lambda w, dil=1: pl.BlockSpec((tm // dil, dil * w), row)
    return pl.pallas_call(
        _merge_kernel,
        grid=(m // tm,),
        in_specs=[rs(D_MODEL), rs(512), rs(512, 4), rs(512, 16), rs(512), rs(512, 4), rs(512, 16),
                  rs(512), rs(1024), rs(UG_W),
                  pl.BlockSpec((512, D_MODEL), const), pl.BlockSpec((512, D_MODEL), const),
                  pl.BlockSpec((1024, D_MODEL), const), pl.BlockSpec((D_MODEL, D_MODEL), const)],
        out_specs=rs(D_MODEL),
        out_shape=jax.ShapeDtypeStruct((m, D_MODEL), F32),
        scratch_shapes=[pltpu.VMEM((4, 4, tm, LANES), F32)],
        compiler_params=_cparams(("parallel",)),
        name="merge",
    )(x2, *o_list, *lse_list, yb, yc, ug, wa, wb, wc, wo)


def _ffn_kernel(x_ref, g_ref, w1_ref, w2_ref, gf_ref, out_ref, *, final_norm):
    x = x_ref[...]
    h = _rms(x, g_ref[...]).astype(BF16)
    acc = x
    for c in range(D_FF // 1024):
        sl = slice(c * 1024, (c + 1) * 1024)
        a = jnp.maximum(jnp.dot(h, w1_ref[:, sl], preferred_element_type=F32), 0.0)
        acc = acc + jnp.dot((a * a).astype(BF16), w2_ref[sl, :], preferred_element_type=F32)
    out_ref[...] = _rms(acc, gf_ref[...]) if final_norm else acc


def _ffn(x2, gain, w1, w2, gain_final, final_norm, tm):
    m = x2.shape[0]
    row = lambda i: (i, 0)
    const = lambda i: (0, 0)
    return pl.pallas_call(
        functools.partial(_ffn_kernel, final_norm=final_norm),
        grid=(m // tm,),
        in_specs=[pl.BlockSpec((tm, D_MODEL), row), pl.BlockSpec((1, D_MODEL), const),
                  pl.BlockSpec((D_MODEL, D_FF), const), pl.BlockSpec((D_FF, D_MODEL), const),
                  pl.BlockSpec((1, D_MODEL), const)],
        out_specs=pl.BlockSpec((tm, D_MODEL), row),
        out_shape=jax.ShapeDtypeStruct((m, D_MODEL), F32),
        compiler_params=_cparams(("parallel",)),
        name="ffn",
    )(x2, gain, w1, w2, gain_final)


def _rope_tables(positions):
    half = ROT_DIM // 2
    inv_freq = jnp.power(ROPE_THETA, -jnp.arange(0, ROT_DIM, 2, dtype=F32) / ROT_DIM)
    ang = positions.astype(F32).reshape(-1, 1) * inv_freq
    cos, sin = jnp.cos(ang), jnp.sin(ang)
    n = ang.shape[0]
    rest = HEAD_DIM - ROT_DIM
    cos64 = jnp.concatenate([cos, cos, jnp.ones((n, rest), F32)], axis=1)
    s1 = jnp.concatenate([-sin, jnp.zeros((n, half + rest), F32)], axis=1)
    s2 = jnp.concatenate([jnp.zeros((n, half), F32), sin, jnp.zeros((n, rest), F32)], axis=1)
    two = lambda t: jnp.concatenate([t, t], axis=1)
    return two(cos64), two(s1), two(s2)


def kernel(x, positions, norm_mix, w_in, b_in, conv_w, a_log, dt_bias, sinks, c_norm, w_branch_a, w_branch_b,
           w_branch_c, w_out, norm_ffn, w_ff1, w_ff2, norm_final):
    bsz, seq, d = x.shape
    depth = w_in.shape[0]
    assert d == D_MODEL and seq % (16 * BLK) == 0
    m = bsz * seq
    cosv, s1, s2 = _rope_tables(positions)
    x2 = x.reshape(m, d)
    ca = 5376
    for layer in range(depth):
        wl, bl = w_in[layer], b_in[layer]
        w_all = jnp.concatenate([wl[:, :ca], wl[:, ca + 16:], wl[:, ca:ca + 16],
                                 jnp.zeros((d, SM_W - 16), F32)], axis=1).astype(BF16)
        b_all = jnp.concatenate([bl[:ca], bl[ca + 16:], bl[ca:ca + 16], jnp.zeros((SM_W - 16,), F32)]).reshape(1, -1)
        ua, ua4, ua16, ub, uc, ug, sm = _inproj(x2, norm_mix[layer].reshape(1, d), w_all, b_all, cosv, s1, s2, tm=256)
        o_list, lse_list = [], []
        for (window, dil), ua_d in zip(A_CONFIGS, (ua, ua4, ua16)):
            o, lse = _attn_a(ua_d, bsz, seq, window, dil)
            o_list.append(o)
            lse_list.append(lse)
        yb = _attn_b(ub, sinks[layer].astype(F32), bsz, seq)
        yc = _gdn(uc, sm, conv_w[layer], a_log[layer], dt_bias[layer], c_norm[layer], bsz, seq)
        x2 = _merge(x2, o_list, lse_list, yb, yc, ug, w_branch_a[layer].astype(BF16), w_branch_b[layer].astype(BF16),
                    w_branch_c[layer].astype(BF16), w_out[layer].astype(BF16), tm=512)
        x2 = _ffn(x2, norm_ffn[layer].reshape(1, d), w_ff1[layer].astype(BF16), w_ff2[layer].astype(BF16),
                  norm_final.reshape(1, d), final_norm=(layer == depth - 1), tm=512)
    return x2.reshape(bsz, seq, d)
```

```python
import functools
import math

import jax
import jax.numpy as jnp
from jax import lax
from jax.experimental import pallas as pl
from jax.experimental.pallas import tpu as pltpu

F32 = jnp.float32
BF16 = jnp.bfloat16

D_MODEL = 1024
HEAD_DIM = 64
ROT_DIM = 16
ROPE_THETA = 500000.0
BLK = 128
NEG_INF = -1e30
EPS = 1e-6
A_CONFIGS = ((128, 1), (512, 4), (2048, 16))
B_WINDOW = 128
C_V_HEADS = 8
C_DK = 128
C_DV = 128
C_CONV = 4
C_CHUNK = 64
D_FF = 4096

LANES = 128
UA_W = 1536
UB_W = 768
UC_W = 3072
UG_W = 3072
SM_W = 128
W_ALL = UA_W + UB_W + UC_W + UG_W + SM_W
VMEM_LIMIT = 56 * 1024 * 1024


def _cparams(sem):
    return pltpu.CompilerParams(dimension_semantics=sem, vmem_limit_bytes=VMEM_LIMIT)


def _rms(x, gain):
    return x * lax.rsqrt(jnp.mean(x * x, axis=-1, keepdims=True) + EPS) * gain


def _inproj_kernel(x_ref, g_ref, w_ref, b_ref, cos_ref, s1_ref, s2_ref, cw_ref,
                   ua_ref, ua4_ref, ua16_ref, ub_ref, uc_ref, ug_ref, sm_ref, stage_ref, tail_ref, *, tiles_per_seq):
    h = _rms(x_ref[...], g_ref[...]).astype(BF16)
    cosv, s1, s2 = cos_ref[...], s1_ref[...], s2_ref[...]

    def seg(off, width):
        return (jnp.dot(h, w_ref[:, off:off + width], preferred_element_type=F32)
                + b_ref[:, off:off + width])

    def rope(a, scale):
        parts = []
        for g in range(a.shape[1] // LANES):
            t = a[:, g * LANES:(g + 1) * LANES]
            r = t * cosv + pltpu.roll(t, LANES - 8, 1) * s1 + pltpu.roll(t, 8, 1) * s2
            parts.append(r * scale if scale != 1.0 else r)
        return jnp.concatenate(parts, axis=1)

    qscale = HEAD_DIM ** -0.5
    tm = x_ref.shape[0]
    for c, val in enumerate((rope(seg(0, 512), qscale), rope(seg(512, 512), 1.0), seg(1024, 512))):
        ua_ref[:, c * 512:(c + 1) * 512] = val.astype(BF16)
        for gl in range(4):
            g = 4 * c + gl
            stage_ref[g] = val[:, gl * LANES:(gl + 1) * LANES]
            for dil, dst in ((4, ua4_ref), (16, ua16_ref)):
                for r in range(dil):
                    col = r * UA_W + g * LANES
                    dst[:, col:col + LANES] = stage_ref[g, pl.ds(r, tm // dil, stride=dil), :].astype(BF16)
    ub_ref[:, 0:512] = rope(seg(1536, 512), qscale).astype(BF16)
    ub_ref[:, 512:640] = rope(seg(2048, 128), 1.0).astype(BF16)
    ub_ref[:, 640:768] = seg(2176, 128).astype(BF16)
    off = UA_W + UB_W

    def silu(t):
        return t * (0.5 + 0.5 * jnp.tanh(0.5 * t))

    @pl.when(pl.program_id(0) % tiles_per_seq == 0)
    def _():
        tail_ref[...] = jnp.zeros_like(tail_ref)

    for c in range(4):
        sl = slice(c * 512, (c + 1) * 512)
        val = seg(off + c * 512, 512)
        ext = jnp.concatenate([tail_ref[:, sl], val], axis=0)
        tail_ref[:, sl] = val[tm - 8:tm]
        y = ext[5:5 + tm] * cw_ref[0:1, sl]
        for j in range(1, C_CONV):
            y = y + ext[5 + j:5 + j + tm] * cw_ref[j:j + 1, sl]
        y = silu(y)
        if c < 2:
            scale = C_DK ** -0.5 if c == 0 else 1.0
            y = jnp.concatenate(
                [t * (lax.rsqrt(jnp.sum(t * t, axis=-1, keepdims=True) + EPS) * scale)
                 for t in (y[:, g * C_DK:(g + 1) * C_DK] for g in range(4))], axis=1)
        uc_ref[:, sl] = y.astype(BF16)
    for c in range(4, 6):
        uc_ref[:, c * 512:(c + 1) * 512] = silu(seg(off + c * 512, 512)).astype(BF16)
    off += UC_W
    for c in range(6):
        ug_ref[:, c * 512:(c + 1) * 512] = jax.nn.sigmoid(seg(off + c * 512, 512)).astype(BF16)
    off += UG_W
    sm_ref[...] = seg(off, SM_W)


def _inproj(x2, gain, w_all, b_all, cosv, s1, s2, conv_w, seq, tm):
    m = x2.shape[0]
    row = lambda i: (i, 0)
    const = lambda i: (0, 0)
    return pl.pallas_call(
        functools.partial(_inproj_kernel, tiles_per_seq=seq // tm),
        grid=(m // tm,),
        in_specs=[
            pl.BlockSpec((tm, D_MODEL), row),
            pl.BlockSpec((1, D_MODEL), const),
            pl.BlockSpec((D_MODEL, W_ALL), const),
            pl.BlockSpec((1, W_ALL), const),
            pl.BlockSpec((tm, LANES), row),
            pl.BlockSpec((tm, LANES), row),
            pl.BlockSpec((tm, LANES), row),
            pl.BlockSpec((C_CONV, 2048), const),
        ],
        out_specs=[
            pl.BlockSpec((tm, UA_W), row),
            pl.BlockSpec((tm // 4, 4 * UA_W), row),
            pl.BlockSpec((tm // 16, 16 * UA_W), row),
            pl.BlockSpec((tm, UB_W), row),
            pl.BlockSpec((tm, UC_W), row),
            pl.BlockSpec((tm, UG_W), row),
            pl.BlockSpec((tm, SM_W), row),
        ],
        out_shape=[
            jax.ShapeDtypeStruct((m, UA_W), BF16),
            jax.ShapeDtypeStruct((m // 4, 4 * UA_W), BF16),
            jax.ShapeDtypeStruct((m // 16, 16 * UA_W), BF16),
            jax.ShapeDtypeStruct((m, UB_W), BF16),
            jax.ShapeDtypeStruct((m, UC_W), BF16),
            jax.ShapeDtypeStruct((m, UG_W), BF16),
            jax.ShapeDtypeStruct((m, SM_W), F32),
        ],
        scratch_shapes=[pltpu.VMEM((UA_W // LANES, tm, LANES), F32),
                        pltpu.VMEM((8, 2048), F32)],
        compiler_params=_cparams(("arbitrary",)),
        name="inproj",
    )(x2, gain, w_all, b_all, cosv, s1, s2, conv_w)


ATT_QT = 512


def _band_valid(max_dist, first_block):
    qi = lax.broadcasted_iota(jnp.int32, (BLK, 2 * BLK), 0)
    kj = lax.broadcasted_iota(jnp.int32, (BLK, 2 * BLK), 1)
    dist = BLK + qi - kj
    band = (dist >= 0) & (dist <= max_dist)
    if first_block is None:
        return band
    return band & (jnp.logical_not(first_block) | (kj >= BLK))


def _kv_window(cur_ref, prev_ref, i, sl):
    if i == 0:
        return jnp.concatenate([prev_ref[0, :, sl], cur_ref[0, 0:BLK, sl]], axis=0)
    return cur_ref[0, (i - 1) * BLK:(i + 1) * BLK, sl]


def _attend_pair(qp, kcat, vcat, valid, sinks):
    lane = lax.broadcasted_iota(jnp.int32, (BLK, LANES), 1)
    first = lane < HEAD_DIM
    outs, lses = [], []
    for hh in range(2):
        qm = jnp.where(first if hh == 0 else ~first, qp, jnp.zeros_like(qp))
        s = lax.dot_general(qm, kcat, (((1,), (1,)), ((), ())), preferred_element_type=F32)
        s = jnp.where(valid, s, NEG_INF)
        m = jnp.max(s, axis=-1, keepdims=True)
        if sinks is not None:
            m = jnp.maximum(m, sinks[hh])
        p = jnp.exp(s - m)
        den = jnp.sum(p, axis=-1, keepdims=True)
        if sinks is not None:
            den = den + jnp.exp(sinks[hh] - m)
        o = jnp.dot(p.astype(BF16), vcat, preferred_element_type=F32)
        outs.append(o / den)
        lses.append(m + jnp.log(den))
    return jnp.where(first, outs[0], outs[1]), jnp.where(first, lses[0], lses[1])


def _attn_a_kernel(q_ref, kc_ref, kp_ref, vc_ref, vp_ref, o_ref, lse_ref, *, max_dist):
    first_tile = pl.program_id(2) == 0
    for i in range(q_ref.shape[1] // BLK):
        valid = _band_valid(max_dist, first_tile if i == 0 else None)
        rows = slice(i * BLK, (i + 1) * BLK)
        for p in range(4):
            sl = slice(p * LANES, (p + 1) * LANES)
            o, lse = _attend_pair(q_ref[0, rows, sl], _kv_window(kc_ref, kp_ref, i, sl),
                                  _kv_window(vc_ref, vp_ref, i, sl), valid, None)
            o_ref[0, rows, sl] = o.astype(BF16)
            lse_ref[0, rows, sl] = lse


def _attn_a(ua_d, bsz, seq, window, dil):
    l = seq // dil
    steps = window // dil
    view = ua_d.reshape(bsz, l, dil * UA_W)
    qt = min(ATT_QT, l)
    blk = (1, qt, 512)
    pblk = (1, BLK, 512)
    cur = lambda c: (lambda b, r, j: (b, j, 3 * r + c))
    prev = lambda c: (lambda b, r, j: (b, jnp.maximum(j * (qt // BLK) - 1, 0), 3 * r + c))
    out = lambda b, r, j: (b, j, r)
    o, lse = pl.pallas_call(
        functools.partial(_attn_a_kernel, max_dist=steps),
        grid=(bsz, dil, l // qt),
        in_specs=[pl.BlockSpec(blk, cur(0)), pl.BlockSpec(blk, cur(1)), pl.BlockSpec(pblk, prev(1)),
                  pl.BlockSpec(blk, cur(2)), pl.BlockSpec(pblk, prev(2))],
        out_specs=[pl.BlockSpec(blk, out), pl.BlockSpec(blk, out)],
        out_shape=[jax.ShapeDtypeStruct((bsz, l, dil * 512), BF16),
                   jax.ShapeDtypeStruct((bsz, l, dil * 512), F32)],
        compiler_params=_cparams(("parallel", "parallel", "arbitrary")),
        name=f"attn_a_d{dil}",
    )(view, view, view, view, view)
    return o.reshape(bsz * l, dil * 512), lse.reshape(bsz * l, dil * 512)


def _attn_b_kernel(sink_ref, q_ref, kc_ref, kp_ref, vc_ref, vp_ref, o_ref):
    first_tile = pl.program_id(1) == 0
    lane = lax.broadcasted_iota(jnp.int32, (2 * BLK, LANES), 1)
    first = lane < HEAD_DIM
    for i in range(q_ref.shape[1] // BLK):
        valid = _band_valid(B_WINDOW - 1, first_tile if i == 0 else None)
        rows = slice(i * BLK, (i + 1) * BLK)
        k = _kv_window(kc_ref, kp_ref, i, slice(0, LANES))
        v = _kv_window(vc_ref, vp_ref, i, slice(0, LANES))
        k_sw = pltpu.roll(k, HEAD_DIM, 1)
        v_sw = pltpu.roll(v, HEAD_DIM, 1)
        kdup = (jnp.where(first, k, k_sw), jnp.where(first, k_sw, k))
        vdup = (jnp.where(first, v, v_sw), jnp.where(first, v_sw, v))
        for p in range(4):
            sl = slice(p * LANES, (p + 1) * LANES)
            kv = p // 2
            sinks = (sink_ref[2 * p], sink_ref[2 * p + 1])
            o, _ = _attend_pair(q_ref[0, rows, sl], kdup[kv], vdup[kv], valid, sinks)
            o_ref[0, rows, sl] = o.astype(BF16)


def _attn_b(ub, sinks, bsz, seq):
    view = ub.reshape(bsz, seq, UB_W)
    qt = ATT_QT
    cur = lambda c: (lambda b, j: (b, j, c))
    prev = lambda c: (lambda b, j: (b, jnp.maximum(j * (qt // BLK) - 1, 0), c))
    o = pl.pallas_call(
        _attn_b_kernel,
        grid=(bsz, seq // qt),
        in_specs=[pl.BlockSpec(memory_space=pltpu.SMEM),
                  pl.BlockSpec((1, qt, 512), lambda b, j: (b, j, 0)),
                  pl.BlockSpec((1, qt, LANES), cur(4)), pl.BlockSpec((1, BLK, LANES), prev(4)),
                  pl.BlockSpec((1, qt, LANES), cur(5)), pl.BlockSpec((1, BLK, LANES), prev(5))],
        out_specs=pl.BlockSpec((1, qt, 512), lambda b, j: (b, j, 0)),
        out_shape=jax.ShapeDtypeStruct((bsz, seq, 512), BF16),
        compiler_params=_cparams(("parallel", "arbitrary")),
        name="attn_b",
    )(sinks, view, view, view, view, view)
    return o.reshape(bsz * seq, 512)


def _split3(a):
    hi = a.astype(BF16)
    r = a - hi.astype(F32)
    mid = r.astype(BF16)
    lo = (r - mid.astype(F32)).astype(BF16)
    return hi, mid, lo


def _dot_nt(a, b):
    return lax.dot_general(a, b, (((1,), (1,)), ((), ())), preferred_element_type=F32)


def _dot_tn(a, b):
    return lax.dot_general(a, b, (((0,), (0,)), ((), ())), preferred_element_type=F32)


def _mm(a, b):
    return jnp.dot(a.astype(BF16), b.astype(BF16), preferred_element_type=F32)


GDN_TT = 512
GDN_NCH = GDN_TT // C_CHUNK
GDN_CPI = 2


def _gdn_kernel(cq_ref, ck_ref, cv_ref, zs_ref, sm_ref, alog_ref, dtb_ref, cn_ref, e_ref, lbd_ref,
                y_ref, state_ref, expd_ref, u_ref, w_ref, qd_ref, kd_ref, attn_ref):
    c, tt = C_CHUNK, GDN_TT

    @pl.when(pl.program_id(1) == 0)
    def _():
        state_ref[...] = jnp.zeros_like(state_ref)

    sm = sm_ref[...]
    lane = lax.broadcasted_iota(jnp.int32, (tt, LANES), 1)
    z = sm + dtb_ref[...]
    softplus = jnp.maximum(z, 0.0) + jnp.log1p(jnp.exp(-jnp.abs(z)))
    g = -jnp.exp(alog_ref[...]) * softplus
    beta = jax.nn.sigmoid(sm)
    lbd = lbd_ref[...]
    g_hi, g_mid, g_lo = _split3(g)
    gc = (jnp.dot(lbd, g_hi, preferred_element_type=F32) + jnp.dot(lbd, g_mid, preferred_element_type=F32)
          + jnp.dot(lbd, g_lo, preferred_element_type=F32))
    c_hi, c_mid, c_lo = _split3(jnp.where(lane < C_V_HEADS, gc, beta))
    lane16 = lax.broadcasted_iota(jnp.int32, (tt, LANES), 1)
    pieces = jnp.where(lane16 < 16, c_hi,
                       jnp.where(lane16 < 32, pltpu.roll(c_mid, 16, 1),
                                 jnp.where(lane16 < 48, pltpu.roll(c_lo, 32, 1), jnp.zeros_like(c_hi))))
    for s in range(4):
        sl = slice(s * 512, (s + 1) * 512)
        expd_ref[:, sl] = jnp.dot(pieces, e_ref[:, sl], preferred_element_type=F32)

    ri = lax.broadcasted_iota(jnp.int32, (c, LANES), 0)
    lj = lax.broadcasted_iota(jnp.int32, (c, LANES), 1)
    cj = lj & (c - 1)
    left = lj < c
    tril = ri >= cj
    strict = ri > cj
    eye = (ri == cj).astype(F32)
    bi = lax.broadcasted_iota(jnp.int32, (2 * c, LANES), 0)
    bj = lax.broadcasted_iota(jnp.int32, (2 * c, LANES), 1)
    blockdiag = (bi < c) == (bj < c)

    def bd(m2):
        m16 = m2.astype(BF16)
        return jnp.where(blockdiag, jnp.concatenate([m16, m16], axis=0), jnp.zeros((2 * c, LANES), BF16))

    def chunk_local(i, carry):
        units = []
        for j in range(GDN_CPI):
            r0 = pl.multiple_of((i * GDN_CPI + j) * c, c)
            rows = pl.ds(r0, c)
            for p in range(C_V_HEADS // 2):
                hs = [slice(h * LANES, (h + 1) * LANES) for h in (2 * p, 2 * p + 1)]
                ps = slice(p * C_DK, (p + 1) * C_DK)
                units.append(dict(rows=rows, p=p, hs=hs, q16=cq_ref[rows, ps], k16=ck_ref[rows, ps],
                                  v=[cv_ref[rows, s].astype(F32) for s in hs],
                                  ge=[expd_ref[rows, s] for s in hs],
                                  be=[expd_ref[rows, 1024 + s.start:1024 + s.stop] for s in hs]))
        for un in units:
            q16, k16 = un["q16"], un["k16"]
            un["qn"], un["kn"] = q16.astype(F32), k16.astype(F32)
            un["kq"] = _dot_nt(jnp.concatenate([k16, q16], axis=0),
                               jnp.concatenate([k16, k16], axis=0))
        for un in units:
            ge, be, kq = un["ge"], un["be"], un["kq"]
            grow = jnp.concatenate(ge, axis=0).T[0:c, :]
            diff = jnp.where(left, ge[0], ge[1]) - grow
            dec = jnp.where(tril, jnp.exp(jnp.where(tril, diff, 0.0)), 0.0)
            un["pw"] = jnp.where(strict, kq[0:c] * dec * jnp.where(left, be[0], be[1]), 0.0) * -1.0
            un["attn"] = jnp.where(tril, kq[c:] * dec, 0.0).astype(BF16)
            un["x"] = eye + un["pw"]
        for un in units:
            un["pw"] = _mm(un["pw"], bd(un["pw"]))
        for _ in range(4):
            for un in units:
                r = _mm(jnp.concatenate([un["pw"], un["x"]], axis=0), bd(un["pw"]))
                un["x"] = un["x"] + r[c:]
                un["pw"] = r[0:c]
        for un in units:
            un["x"] = un["x"] + _mm(un["x"], bd(un["pw"]))
        zero = jnp.zeros((c, 2 * C_DV), F32)
        for un in units:
            rhs, un["qd"], un["kd"] = [], [], []
            for ge, be, v in zip(un["ge"], un["be"], un["v"]):
                eg = jnp.exp(ge)
                rhs.append(jnp.concatenate([v * be, un["kn"] * be * eg], axis=1))
                un["qd"].append((un["qn"] * eg).astype(BF16))
                un["kd"].append((un["kn"] * jnp.exp(ge[c - 1:c, :] - ge)).astype(BF16))
            bdr = jnp.concatenate([jnp.concatenate([rhs[0], zero], axis=1),
                                   jnp.concatenate([zero, rhs[1]], axis=1)], axis=0)
            un["uw"] = _mm(un["x"], bdr)
        for un in units:
            rows, p = un["rows"], un["p"]
            for t, s in enumerate(un["hs"]):
                u_ref[rows, s] = un["uw"][:, 2 * t * C_DV:(2 * t + 1) * C_DV]
                w_ref[rows, s] = un["uw"][:, (2 * t + 1) * C_DV:(2 * t + 2) * C_DV].astype(BF16)
                qd_ref[rows, s] = un["qd"][t]
                kd_ref[rows, s] = un["kd"][t]
            attn_ref[rows, p * LANES:(p + 1) * LANES] = un["attn"]
        return carry

    lax.fori_loop(0, GDN_NCH // GDN_CPI, chunk_local, 0)
    cn = cn_ref[...]

    def recur(i, carry):
        r0 = pl.multiple_of(i * c, c)
        rows = pl.ds(r0, c)
        last8 = pl.ds(pl.multiple_of(r0 + c - 8, 8), 8)
        hsl = [slice(h * C_DV, (h + 1) * C_DV) for h in range(C_V_HEADS)]
        heads = range(C_V_HEADS)
        s = [state_ref[h] for h in heads]
        wq = [jnp.dot(jnp.concatenate([w_ref[rows, hsl[h]], qd_ref[rows, hsl[h]]], axis=0), s[h].astype(BF16),
                      preferred_element_type=F32) for h in heads]
        v16 = [(u_ref[rows, hsl[h]] - wq[h][0:c]).astype(BF16) for h in heads]
        s_new = [s[h] * jnp.exp(expd_ref[last8, hsl[h]][7:8]) + _dot_tn(kd_ref[rows, hsl[h]], v16[h]) for h in heads]
        zero = jnp.zeros((c, C_DV), BF16)
        o = []
        for p in range(C_V_HEADS // 2):
            bdv = jnp.concatenate([jnp.concatenate([v16[2 * p], zero], axis=1),
                                   jnp.concatenate([zero, v16[2 * p + 1]], axis=1)], axis=0)
            av = jnp.dot(attn_ref[rows, p * LANES:(p + 1) * LANES], bdv, preferred_element_type=F32)
            o += [wq[2 * p][c:] + av[:, 0:C_DV], wq[2 * p + 1][c:] + av[:, C_DV:]]
        for h in heads:
            state_ref[h] = s_new[h]
            y_ref[rows, hsl[h]] = (_rms(o[h], cn) * zs_ref[rows, hsl[h]].astype(F32)).astype(BF16)
        return carry

    lax.fori_loop(0, GDN_NCH, recur, 0)


def _gdn(uc, sm, a_log, dt_bias, c_norm, bsz, seq):
    c, tt = C_CHUNK, GDN_TT
    nt = seq // tt
    pad = lambda v: jnp.pad(v.astype(F32), (0, LANES - v.shape[0])).reshape(1, LANES)
    li = jnp.arange(LANES)[:, None]
    cj = jnp.arange(2048)[None, :]
    lp = li % 16
    e = ((li < 48) & (((lp < 8) & (cj < 1024) & (cj // LANES == lp))
                      | ((lp >= 8) & (cj >= 1024) & ((cj - 1024) // LANES == lp - 8)))).astype(BF16)
    ti = jnp.arange(tt)
    lbd = ((ti[:, None] // c == ti[None, :] // c) & (ti[:, None] >= ti[None, :])).astype(BF16)
    row = lambda blockcol: (lambda b, t: (b * nt + t, blockcol))
    const = lambda b, t: (0, 0)
    return pl.pallas_call(
        _gdn_kernel,
        grid=(bsz, nt),
        in_specs=[
            pl.BlockSpec((tt, 512), row(0)),
            pl.BlockSpec((tt, 512), row(1)),
            pl.BlockSpec((tt, 1024), row(1)),
            pl.BlockSpec((tt, 1024), row(2)),
            pl.BlockSpec((tt, SM_W), row(0)),
            pl.BlockSpec((1, LANES), const),
            pl.BlockSpec((1, LANES), const),
            pl.BlockSpec((1, LANES), const),
            pl.BlockSpec((LANES, 2048), const),
            pl.BlockSpec((tt, tt), const),
        ],
        out_specs=pl.BlockSpec((tt, 1024), row(0)),
        out_shape=jax.ShapeDtypeStruct((bsz * seq, 1024), BF16),
        scratch_shapes=[
            pltpu.VMEM((C_V_HEADS, C_DK, C_DV), F32),
            pltpu.VMEM((tt, 2048), F32),
            pltpu.VMEM((tt, 1024), F32),
            pltpu.VMEM((tt, 1024), BF16),
            pltpu.VMEM((tt, 1024), BF16),
            pltpu.VMEM((tt, 1024), BF16),
            pltpu.VMEM((tt, 512), BF16),
        ],
        compiler_params=_cparams(("parallel", "arbitrary")),
        name="gdn",
    )(uc, uc, uc, uc, sm, pad(a_log), pad(dt_bias), c_norm.astype(F32).reshape(1, LANES), e, lbd)


def _merge_kernel(x_ref, o1_ref, o2_ref, o3_ref, l1_ref, l2_ref, l3_ref, yb_ref, yc_ref, ug_ref,
                  wa_ref, wb_ref, wc_ref, wo_ref, out_ref, nat_ref):
    tm = x_ref.shape[0]
    for slot, (dil, src) in enumerate(((4, o2_ref), (4, l2_ref), (16, o3_ref), (16, l3_ref))):
        for r in range(dil):
            for g in range(4):
                col = r * 512 + g * LANES
                nat_ref[slot, g, pl.ds(r, tm // dil, stride=dil), :] = src[:, col:col + LANES].astype(F32)
    parts = []
    for g in range(4):
        sl = slice(g * LANES, (g + 1) * LANES)
        l1, l2, l3 = l1_ref[:, sl], nat_ref[1, g], nat_ref[3, g]
        m = jnp.maximum(jnp.maximum(l1, l2), l3)
        e1, e2, e3 = jnp.exp(l1 - m), jnp.exp(l2 - m), jnp.exp(l3 - m)
        parts.append(((e1 * o1_ref[:, sl].astype(F32) + e2 * nat_ref[0, g] + e3 * nat_ref[2, g])
                      / (e1 + e2 + e3)).astype(BF16))
    ya = jnp.concatenate(parts, axis=1)
    merged = (ug_ref[:, 0:1024].astype(F32) * jnp.dot(ya, wa_ref[...], preferred_element_type=F32)
              + ug_ref[:, 1024:2048].astype(F32) * jnp.dot(yb_ref[...], wb_ref[...], preferred_element_type=F32)
              + ug_ref[:, 2048:3072].astype(F32) * jnp.dot(yc_ref[...], wc_ref[...], preferred_element_type=F32))
    out_ref[...] = x_ref[...] + jnp.dot(merged.astype(BF16), wo_ref[...], preferred_element_type=F32)


def _merge(x2, o_list, lse_list, yb, yc, ug, wa, wb, wc, wo, tm):
    m = x2.shape[0]
    row = lambda i: (i, 0)
    const = lambda i: (0, 0)
    rs = lambda w, dil=1: pl.BlockSpec((tm // dil, dil * w), row)
    return pl.pallas_call(
        _merge_kernel,
        grid=(m // tm,),
        in_specs=[rs(D_MODEL), rs(512), rs(512, 4), rs(512, 16), rs(512), rs(512, 4), rs(512, 16),
                  rs(512), rs(1024), rs(UG_W),
                  pl.BlockSpec((512, D_MODEL), const), pl.BlockSpec((512, D_MODEL), const),
                  pl.BlockSpec((1024, D_MODEL), const), pl.BlockSpec((D_MODEL, D_MODEL), const)],
        out_specs=rs(D_MODEL),
        out_shape=jax.ShapeDtypeStruct((m, D_MODEL), F32),
        scratch_shapes=[pltpu.VMEM((4, 4, tm, LANES), F32)],
        compiler_params=_cparams(("parallel",)),
        name="merge",
    )(x2, *o_list, *lse_list, yb, yc, ug, wa, wb, wc, wo)


def _ffn_kernel(x_ref, g_ref, w1_ref, w2_ref, gf_ref, out_ref, *, final_norm):
    x = x_ref[...]
    h = _rms(x, g_ref[...]).astype(BF16)
    acc = x
    for c in range(D_FF // 1024):
        sl = slice(c * 1024, (c + 1) * 1024)
        a = jnp.maximum(jnp.dot(h, w1_ref[:, sl], preferred_element_type=F32), 0.0)
        acc = acc + jnp.dot((a * a).astype(BF16), w2_ref[sl, :], preferred_element_type=F32)
    out_ref[...] = _rms(acc, gf_ref[...]) if final_norm else acc


def _ffn(x2, gain, w1, w2, gain_final, final_norm, tm):
    m = x2.shape[0]
    row = lambda i: (i, 0)
    const = lambda i: (0, 0)
    return pl.pallas_call(
        functools.partial(_ffn_kernel, final_norm=final_norm),
        grid=(m // tm,),
        in_specs=[pl.BlockSpec((tm, D_MODEL), row), pl.BlockSpec((1, D_MODEL), const),
                  pl.BlockSpec((D_MODEL, D_FF), const), pl.BlockSpec((D_FF, D_MODEL), const),
                  pl.BlockSpec((1, D_MODEL), const)],
        out_specs=pl.BlockSpec((tm, D_MODEL), row),
        out_shape=jax.ShapeDtypeStruct((m, D_MODEL), F32),
        compiler_params=_cparams(("parallel",)),
        name="ffn",
    )(x2, gain, w1, w2, gain_final)


def _rope_tables(positions):
    half = ROT_DIM // 2
    inv_freq = jnp.power(ROPE_THETA, -jnp.arange(0, ROT_DIM, 2, dtype=F32) / ROT_DIM)
    ang = positions.astype(F32).reshape(-1, 1) * inv_freq
    cos, sin = jnp.cos(ang), jnp.sin(ang)
    n = ang.shape[0]
    rest = HEAD_DIM - ROT_DIM
    cos64 = jnp.concatenate([cos, cos, jnp.ones((n, rest), F32)], axis=1)
    s1 = jnp.concatenate([-sin, jnp.zeros((n, half + rest), F32)], axis=1)
    s2 = jnp.concatenate([jnp.zeros((n, half), F32), sin, jnp.zeros((n, rest), F32)], axis=1)
    two = lambda t: jnp.concatenate([t, t], axis=1)
    return two(cos64), two(s1), two(s2)


def kernel(x, positions, norm_mix, w_in, b_in, conv_w, a_log, dt_bias, sinks, c_norm, w_branch_a, w_branch_b,
           w_branch_c, w_out, norm_ffn, w_ff1, w_ff2, norm_final):
    bsz, seq, d = x.shape
    depth = w_in.shape[0]
    assert d == D_MODEL and seq % (16 * BLK) == 0
    m = bsz * seq
    cosv, s1, s2 = _rope_tables(positions)
    x2 = x.reshape(m, d)
    ca = 5376
    for layer in range(depth):
        wl, bl = w_in[layer], b_in[layer]
        w_all = jnp.concatenate([wl[:, :ca], wl[:, ca + 16:], wl[:, ca:ca + 16],
                                 jnp.zeros((d, SM_W - 16), F32)], axis=1).astype(BF16)
        b_all = jnp.concatenate([bl[:ca], bl[ca + 16:], bl[ca:ca + 16], jnp.zeros((SM_W - 16,), F32)]).reshape(1, -1)
        ua, ua4, ua16, ub, uc, ug, sm = _inproj(x2, norm_mix[layer].reshape(1, d), w_all, b_all, cosv, s1, s2,
                                                conv_w[layer].astype(F32), seq, tm=256)
        o_list, lse_list = [], []
        for (window, dil), ua_d in zip(A_CONFIGS, (ua, ua4, ua16)):
            o, lse = _attn_a(ua_d, bsz, seq, window, dil)
            o_list.append(o)
            lse_list.append(lse)
        yb = _attn_b(ub, sinks[layer].astype(F32), bsz, seq)
        yc = _gdn(uc, sm, a_log[layer], dt_bias[layer], c_norm[layer], bsz, seq)
        x2 = _merge(x2, o_list, lse_list, yb, yc, ug, w_branch_a[layer].astype(BF16), w_branch_b[layer].astype(BF16),
                    w_branch_c[layer].astype(BF16), w_out[layer].astype(BF16), tm=512)
        x2 = _ffn(x2, norm_ffn[layer].reshape(1, d), w_ff1[layer].astype(BF16), w_ff2[layer].astype(BF16),
                  norm_final.reshape(1, d), final_norm=(layer == depth - 1), tm=512)
    return x2.reshape(bsz, seq, d)
```

```python
import functools
import math

import jax
import jax.numpy as jnp
from jax import lax
from jax.experimental import pallas as pl
from jax.experimental.pallas import tpu as pltpu

F32 = jnp.float32
BF16 = jnp.bfloat16

D_MODEL = 1024
HEAD_DIM = 64
ROT_DIM = 16
ROPE_THETA = 500000.0
BLK = 128
NEG_INF = -1e30
EPS = 1e-6
A_CONFIGS = ((128, 1), (512, 4), (2048, 16))
B_WINDOW = 128
C_V_HEADS = 8
C_DK = 128
C_DV = 128
C_CONV = 4
C_CHUNK = 64
D_FF = 4096

LANES = 128
UA_W = 1536
UB_W = 768
UC_W = 3072
UG_W = 3072
SM_W = 128
W_ALL = UA_W + UB_W + UC_W + UG_W + SM_W
VMEM_LIMIT = 56 * 1024 * 1024


def _cparams(sem):
    return pltpu.CompilerParams(dimension_semantics=sem, vmem_limit_bytes=VMEM_LIMIT)


def _rms(x, gain):
    return x * lax.rsqrt(jnp.mean(x * x, axis=-1, keepdims=True) + EPS) * gain


def _inproj_kernel(x_ref, g_ref, w_ref, b_ref, cos_ref, s1_ref, s2_ref, cw_ref,
                   ua_ref, ua4_ref, ua16_ref, ub_ref, uc_ref, ug_ref, sm_ref, stage_ref, ext_ref, h_ref,
                   *, tiles_per_seq):
    h_ref[...] = _rms(x_ref[...], g_ref[...]).astype(BF16)
    cosv, s1, s2 = cos_ref[...], s1_ref[...], s2_ref[...]

    def seg(off, width):
        return (jnp.dot(h_ref[...], w_ref[:, off:off + width], preferred_element_type=F32)
                + b_ref[:, off:off + width])

    def rope(a, scale):
        parts = []
        for g in range(a.shape[1] // LANES):
            t = a[:, g * LANES:(g + 1) * LANES]
            r = t * cosv + pltpu.roll(t, LANES - 8, 1) * s1 + pltpu.roll(t, 8, 1) * s2
            parts.append(r * scale if scale != 1.0 else r)
        return jnp.concatenate(parts, axis=1)

    qscale = HEAD_DIM ** -0.5
    tm = x_ref.shape[0]

    def seg_a(c):
        val = seg(c * 512, 512)
        if c < 2:
            val = rope(val, qscale if c == 0 else 1.0)
        ua_ref[:, c * 512:(c + 1) * 512] = val.astype(BF16)
        for gl in range(4):
            g = 4 * c + gl
            stage_ref[g] = val[:, gl * LANES:(gl + 1) * LANES]
            for dil, dst in ((4, ua4_ref), (16, ua16_ref)):
                for r in range(dil):
                    col = r * UA_W + g * LANES
                    dst[:, col:col + LANES] = stage_ref[g, pl.ds(r, tm // dil, stride=dil), :].astype(BF16)

    def seg_b():
        ub_ref[:, 0:512] = rope(seg(UA_W, 512), qscale).astype(BF16)
        ub_ref[:, 512:640] = rope(seg(UA_W + 512, 128), 1.0).astype(BF16)
        ub_ref[:, 640:768] = seg(UA_W + 640, 128).astype(BF16)

    def silu(t):
        return t * (0.5 + 0.5 * jnp.tanh(0.5 * t))

    @pl.when(pl.program_id(0) % tiles_per_seq == 0)
    def _():
        ext_ref[:, 0:8, :] = jnp.zeros((ext_ref.shape[0], 8, LANES), F32)

    def seg_conv(c):
        val = seg(UA_W + UB_W + c * 512, 512)
        for gl in range(4):
            g = 4 * c + gl
            sl = slice(g * LANES, (g + 1) * LANES)
            cur = val[:, gl * LANES:(gl + 1) * LANES]
            ext_ref[g, 8:8 + tm, :] = cur
            y = cur * cw_ref[C_CONV - 1:C_CONV, sl]
            for j in range(C_CONV - 1):
                y = y + ext_ref[g, 5 + j:5 + j + tm, :] * cw_ref[j:j + 1, sl]
            ext_ref[g, 0:8, :] = cur[tm - 8:tm]
            y = silu(y)
            if c < 2:
                scale = C_DK ** -0.5 if c == 0 else 1.0
                y = y * (lax.rsqrt(jnp.sum(y * y, axis=-1, keepdims=True) + EPS) * scale)
            uc_ref[:, sl] = y.astype(BF16)

    def seg_z(c):
        uc_ref[:, c * 512:(c + 1) * 512] = silu(seg(UA_W + UB_W + c * 512, 512)).astype(BF16)

    def seg_gate(c):
        ug_ref[:, c * 256:(c + 1) * 256] = jax.nn.sigmoid(seg(UA_W + UB_W + UC_W + c * 256, 256)).astype(BF16)

    def seg_small():
        sm_ref[...] = seg(UA_W + UB_W + UC_W + UG_W, SM_W)

    heavy = ([functools.partial(seg_a, c) for c in range(3)] + [seg_b]
             + [functools.partial(seg_conv, c) for c in range(4)]
             + [functools.partial(seg_z, c) for c in (4, 5)] + [seg_small])
    gates = [functools.partial(seg_gate, c) for c in range(12)]
    for k, fn in enumerate(heavy):
        fn()
        for gfn in gates[k * len(gates) // len(heavy):(k + 1) * len(gates) // len(heavy)]:
            gfn()


def _inproj(x2, gain, w_all, b_all, cosv, s1, s2, conv_w, seq, tm):
    m = x2.shape[0]
    row = lambda i: (i, 0)
    const = lambda i: (0, 0)
    return pl.pallas_call(
        functools.partial(_inproj_kernel, tiles_per_seq=seq // tm),
        grid=(m // tm,),
        in_specs=[
            pl.BlockSpec((tm, D_MODEL), row),
            pl.BlockSpec((1, D_MODEL), const),
            pl.BlockSpec((D_MODEL, W_ALL), const),
            pl.BlockSpec((1, W_ALL), const),
            pl.BlockSpec((tm, LANES), row),
            pl.BlockSpec((tm, LANES), row),
            pl.BlockSpec((tm, LANES), row),
            pl.BlockSpec((C_CONV, 2048), const),
        ],
        out_specs=[
            pl.BlockSpec((tm, UA_W), row),
            pl.BlockSpec((tm // 4, 4 * UA_W), row),
            pl.BlockSpec((tm // 16, 16 * UA_W), row),
            pl.BlockSpec((tm, UB_W), row),
            pl.BlockSpec((tm, UC_W), row),
            pl.BlockSpec((tm, UG_W), row),
            pl.BlockSpec((tm, SM_W), row),
        ],
        out_shape=[
            jax.ShapeDtypeStruct((m, UA_W), BF16),
            jax.ShapeDtypeStruct((m // 4, 4 * UA_W), BF16),
            jax.ShapeDtypeStruct((m // 16, 16 * UA_W), BF16),
            jax.ShapeDtypeStruct((m, UB_W), BF16),
            jax.ShapeDtypeStruct((m, UC_W), BF16),
            jax.ShapeDtypeStruct((m, UG_W), BF16),
            jax.ShapeDtypeStruct((m, SM_W), F32),
        ],
        scratch_shapes=[pltpu.VMEM((UA_W // LANES, tm, LANES), F32),
                        pltpu.VMEM((2048 // LANES, tm + 8, LANES), F32),
                        pltpu.VMEM((tm, D_MODEL), BF16)],
        compiler_params=_cparams(("arbitrary",)),
        name="inproj",
    )(x2, gain, w_all, b_all, cosv, s1, s2, conv_w)


ATT_QT = 512


def _band_valid(max_dist, first_block):
    qi = lax.broadcasted_iota(jnp.int32, (BLK, 2 * BLK), 0)
    kj = lax.broadcasted_iota(jnp.int32, (BLK, 2 * BLK), 1)
    dist = BLK + qi - kj
    band = (dist >= 0) & (dist <= max_dist)
    if first_block is None:
        return band
    return band & (jnp.logical_not(first_block) | (kj >= BLK))


def _kv_window(cur_ref, prev_ref, i, sl):
    if i == 0:
        return jnp.concatenate([prev_ref[0, :, sl], cur_ref[0, 0:BLK, sl]], axis=0)
    return cur_ref[0, (i - 1) * BLK:(i + 1) * BLK, sl]


def _attend_pair(qp, kcat, vcat, valid, sinks):
    lane = lax.broadcasted_iota(jnp.int32, (BLK, LANES), 1)
    first = lane < HEAD_DIM
    outs, lses = [], []
    for hh in range(2):
        qm = jnp.where(first if hh == 0 else ~first, qp, jnp.zeros_like(qp))
        s = lax.dot_general(qm, kcat, (((1,), (1,)), ((), ())), preferred_element_type=F32)
        s = jnp.where(valid, s, NEG_INF)
        m = jnp.max(s, axis=-1, keepdims=True)
        if sinks is not None:
            m = jnp.maximum(m, sinks[hh])
        p = jnp.exp(s - m)
        den = jnp.sum(p, axis=-1, keepdims=True)
        if sinks is not None:
            den = den + jnp.exp(sinks[hh] - m)
        o = jnp.dot(p.astype(BF16), vcat, preferred_element_type=F32)
        outs.append(o / den)
        lses.append(m + jnp.log(den))
    return jnp.where(first, outs[0], outs[1]), jnp.where(first, lses[0], lses[1])


def _attn_a_kernel(q_ref, kc_ref, kp_ref, vc_ref, vp_ref, o_ref, lse_ref, *, max_dist):
    first_tile = pl.program_id(2) == 0
    for i in range(q_ref.shape[1] // BLK):
        valid = _band_valid(max_dist, first_tile if i == 0 else None)
        rows = slice(i * BLK, (i + 1) * BLK)
        for p in range(4):
            sl = slice(p * LANES, (p + 1) * LANES)
            o, lse = _attend_pair(q_ref[0, rows, sl], _kv_window(kc_ref, kp_ref, i, sl),
                                  _kv_window(vc_ref, vp_ref, i, sl), valid, None)
            o_ref[0, rows, sl] = o.astype(BF16)
            lse_ref[0, rows, sl] = lse


def _attn_a(ua_d, bsz, seq, window, dil):
    l = seq // dil
    steps = window // dil
    view = ua_d.reshape(bsz, l, dil * UA_W)
    qt = min(ATT_QT, l)
    blk = (1, qt, 512)
    pblk = (1, BLK, 512)
    cur = lambda c: (lambda b, r, j: (b, j, 3 * r + c))
    prev = lambda c: (lambda b, r, j: (b, jnp.maximum(j * (qt // BLK) - 1, 0), 3 * r + c))
    out = lambda b, r, j: (b, j, r)
    o, lse = pl.pallas_call(
        functools.partial(_attn_a_kernel, max_dist=steps),
        grid=(bsz, dil, l // qt),
        in_specs=[pl.BlockSpec(blk, cur(0)), pl.BlockSpec(blk, cur(1)), pl.BlockSpec(pblk, prev(1)),
                  pl.BlockSpec(blk, cur(2)), pl.BlockSpec(pblk, prev(2))],
        out_specs=[pl.BlockSpec(blk, out), pl.BlockSpec(blk, out)],
        out_shape=[jax.ShapeDtypeStruct((bsz, l, dil * 512), BF16),
                   jax.ShapeDtypeStruct((bsz, l, dil * 512), F32)],
        compiler_params=_cparams(("parallel", "parallel", "arbitrary")),
        name=f"attn_a_d{dil}",
    )(view, view, view, view, view)
    return o.reshape(bsz * l, dil * 512), lse.reshape(bsz * l, dil * 512)


def _attn_b_kernel(sink_ref, q_ref, kc_ref, kp_ref, vc_ref, vp_ref, o_ref):
    first_tile = pl.program_id(1) == 0
    lane = lax.broadcasted_iota(jnp.int32, (2 * BLK, LANES), 1)
    first = lane < HEAD_DIM
    for i in range(q_ref.shape[1] // BLK):
        valid = _band_valid(B_WINDOW - 1, first_tile if i == 0 else None)
        rows = slice(i * BLK, (i + 1) * BLK)
        k = _kv_window(kc_ref, kp_ref, i, slice(0, LANES))
        v = _kv_window(vc_ref, vp_ref, i, slice(0, LANES))
        k_sw = pltpu.roll(k, HEAD_DIM, 1)
        v_sw = pltpu.roll(v, HEAD_DIM, 1)
        kdup = (jnp.where(first, k, k_sw), jnp.where(first, k_sw, k))
        vdup = (jnp.where(first, v, v_sw), jnp.where(first, v_sw, v))
        for p in range(4):
            sl = slice(p * LANES, (p + 1) * LANES)
            kv = p // 2
            sinks = (sink_ref[2 * p], sink_ref[2 * p + 1])
            o, _ = _attend_pair(q_ref[0, rows, sl], kdup[kv], vdup[kv], valid, sinks)
            o_ref[0, rows, sl] = o.astype(BF16)


def _attn_b(ub, sinks, bsz, seq):
    view = ub.reshape(bsz, seq, UB_W)
    qt = ATT_QT
    cur = lambda c: (lambda b, j: (b, j, c))
    prev = lambda c: (lambda b, j: (b, jnp.maximum(j * (qt // BLK) - 1, 0), c))
    o = pl.pallas_call(
        _attn_b_kernel,
        grid=(bsz, seq // qt),
        in_specs=[pl.BlockSpec(memory_space=pltpu.SMEM),
                  pl.BlockSpec((1, qt, 512), lambda b, j: (b, j, 0)),
                  pl.BlockSpec((1, qt, LANES), cur(4)), pl.BlockSpec((1, BLK, LANES), prev(4)),
                  pl.BlockSpec((1, qt, LANES), cur(5)), pl.BlockSpec((1, BLK, LANES), prev(5))],
        out_specs=pl.BlockSpec((1, qt, 512), lambda b, j: (b, j, 0)),
        out_shape=jax.ShapeDtypeStruct((bsz, seq, 512), BF16),
        compiler_params=_cparams(("parallel", "arbitrary")),
        name="attn_b",
    )(sinks, view, view, view, view, view)
    return o.reshape(bsz * seq, 512)


def _split3(a):
    hi = a.astype(BF16)
    r = a - hi.astype(F32)
    mid = r.astype(BF16)
    lo = (r - mid.astype(F32)).astype(BF16)
    return hi, mid, lo


def _dot_nt(a, b):
    return lax.dot_general(a, b, (((1,), (1,)), ((), ())), preferred_element_type=F32)


def _dot_tn(a, b):
    return lax.dot_general(a, b, (((0,), (0,)), ((), ())), preferred_element_type=F32)


def _mm(a, b):
    return jnp.dot(a.astype(BF16), b.astype(BF16), preferred_element_type=F32)


GDN_TT = 512
GDN_NCH = GDN_TT // C_CHUNK
GDN_CPI = 2


def _gdn_kernel(cq_ref, ck_ref, cv_ref, zs_ref, sm_ref, alog_ref, dtb_ref, cn_ref, e_ref, lbd_ref,
                y_ref, state_ref, expd_ref, u_ref, w_ref, qd_ref, kd_ref, attn_ref):
    c, tt = C_CHUNK, GDN_TT

    @pl.when(pl.program_id(1) == 0)
    def _():
        state_ref[...] = jnp.zeros_like(state_ref)

    sm = sm_ref[...]
    lane = lax.broadcasted_iota(jnp.int32, (tt, LANES), 1)
    z = sm + dtb_ref[...]
    softplus = jnp.maximum(z, 0.0) + jnp.log1p(jnp.exp(-jnp.abs(z)))
    g = -jnp.exp(alog_ref[...]) * softplus
    beta = jax.nn.sigmoid(sm)
    lbd = lbd_ref[...]
    g_hi, g_mid, g_lo = _split3(g)
    gc = (jnp.dot(lbd, g_hi, preferred_element_type=F32) + jnp.dot(lbd, g_mid, preferred_element_type=F32)
          + jnp.dot(lbd, g_lo, preferred_element_type=F32))
    c_hi, c_mid, c_lo = _split3(jnp.where(lane < C_V_HEADS, gc, beta))
    lane16 = lax.broadcasted_iota(jnp.int32, (tt, LANES), 1)
    pieces = jnp.where(lane16 < 16, c_hi,
                       jnp.where(lane16 < 32, pltpu.roll(c_mid, 16, 1),
                                 jnp.where(lane16 < 48, pltpu.roll(c_lo, 32, 1), jnp.zeros_like(c_hi))))
    for s in range(4):
        sl = slice(s * 512, (s + 1) * 512)
        expd_ref[:, sl] = jnp.dot(pieces, e_ref[:, sl], preferred_element_type=F32)

    ri = lax.broadcasted_iota(jnp.int32, (c, LANES), 0)
    lj = lax.broadcasted_iota(jnp.int32, (c, LANES), 1)
    cj = lj & (c - 1)
    left = lj < c
    tril = ri >= cj
    strict = ri > cj
    eye = (ri == cj).astype(F32)
    bi = lax.broadcasted_iota(jnp.int32, (2 * c, LANES), 0)
    bj = lax.broadcasted_iota(jnp.int32, (2 * c, LANES), 1)
    blockdiag = (bi < c) == (bj < c)

    def bd(m2):
        m16 = m2.astype(BF16)
        return jnp.where(blockdiag, jnp.concatenate([m16, m16], axis=0), jnp.zeros((2 * c, LANES), BF16))

    def chunk_local(i, carry):
        units = []
        for j in range(GDN_CPI):
            r0 = pl.multiple_of((i * GDN_CPI + j) * c, c)
            rows = pl.ds(r0, c)
            for p in range(C_V_HEADS // 2):
                hs = [slice(h * LANES, (h + 1) * LANES) for h in (2 * p, 2 * p + 1)]
                ps = slice(p * C_DK, (p + 1) * C_DK)
                units.append(dict(rows=rows, p=p, hs=hs, q16=cq_ref[rows, ps], k16=ck_ref[rows, ps],
                                  v=[cv_ref[rows, s].astype(F32) for s in hs],
                                  ge=[expd_ref[rows, s] for s in hs],
                                  be=[expd_ref[rows, 1024 + s.start:1024 + s.stop] for s in hs]))
        for un in units:
            q16, k16 = un["q16"], un["k16"]
            un["qn"], un["kn"] = q16.astype(F32), k16.astype(F32)
            un["kq"] = _dot_nt(jnp.concatenate([k16, q16], axis=0),
                               jnp.concatenate([k16, k16], axis=0))
        for un in units:
            ge, be, kq = un["ge"], un["be"], un["kq"]
            grow = jnp.concatenate(ge, axis=0).T[0:c, :]
            diff = jnp.where(left, ge[0], ge[1]) - grow
            dec = jnp.where(tril, jnp.exp(jnp.where(tril, diff, 0.0)), 0.0)
            un["pw"] = jnp.where(strict, kq[0:c] * dec * jnp.where(left, be[0], be[1]), 0.0) * -1.0
            un["attn"] = jnp.where(tril, kq[c:] * dec, 0.0).astype(BF16)
            un["x"] = eye + un["pw"]
        for un in units:
            un["pw"] = _mm(un["pw"], bd(un["pw"]))
        for _ in range(4):
            for un in units:
                r = _mm(jnp.concatenate([un["pw"], un["x"]], axis=0), bd(un["pw"]))
                un["x"] = un["x"] + r[c:]
                un["pw"] = r[0:c]
        for un in units:
            un["x"] = un["x"] + _mm(un["x"], bd(un["pw"]))
        zero = jnp.zeros((c, 2 * C_DV), F32)
        for un in units:
            rhs, un["qd"], un["kd"] = [], [], []
            for ge, be, v in zip(un["ge"], un["be"], un["v"]):
                eg = jnp.exp(ge)
                rhs.append(jnp.concatenate([v * be, un["kn"] * be * eg], axis=1))
                un["qd"].append((un["qn"] * eg).astype(BF16))
                un["kd"].append((un["kn"] * jnp.exp(ge[c - 1:c, :] - ge)).astype(BF16))
            bdr = jnp.concatenate([jnp.concatenate([rhs[0], zero], axis=1),
                                   jnp.concatenate([zero, rhs[1]], axis=1)], axis=0)
            un["uw"] = _mm(un["x"], bdr)
        for un in units:
            rows, p = un["rows"], un["p"]
            for t, s in enumerate(un["hs"]):
                u_ref[rows, s] = un["uw"][:, 2 * t * C_DV:(2 * t + 1) * C_DV]
                w_ref[rows, s] = un["uw"][:, (2 * t + 1) * C_DV:(2 * t + 2) * C_DV].astype(BF16)
                qd_ref[rows, s] = un["qd"][t]
                kd_ref[rows, s] = un["kd"][t]
            attn_ref[rows, p * LANES:(p + 1) * LANES] = un["attn"]
        return carry

    lax.fori_loop(0, GDN_NCH // GDN_CPI, chunk_local, 0)
    cn = cn_ref[...]

    def recur(i, carry):
        r0 = pl.multiple_of(i * c, c)
        rows = pl.ds(r0, c)
        last8 = pl.ds(pl.multiple_of(r0 + c - 8, 8), 8)
        hsl = [slice(h * C_DV, (h + 1) * C_DV) for h in range(C_V_HEADS)]
        heads = range(C_V_HEADS)
        s = [state_ref[h] for h in heads]
        wq = [jnp.dot(jnp.concatenate([w_ref[rows, hsl[h]], qd_ref[rows, hsl[h]]], axis=0), s[h].astype(BF16),
                      preferred_element_type=F32) for h in heads]
        v16 = [(u_ref[rows, hsl[h]] - wq[h][0:c]).astype(BF16) for h in heads]
        s_new = [s[h] * jnp.exp(expd_ref[last8, hsl[h]][7:8]) + _dot_tn(kd_ref[rows, hsl[h]], v16[h]) for h in heads]
        zero = jnp.zeros((c, C_DV), BF16)
        o = []
        for p in range(C_V_HEADS // 2):
            bdv = jnp.concatenate([jnp.concatenate([v16[2 * p], zero], axis=1),
                                   jnp.concatenate([zero, v16[2 * p + 1]], axis=1)], axis=0)
            av = jnp.dot(attn_ref[rows, p * LANES:(p + 1) * LANES], bdv, preferred_element_type=F32)
            o += [wq[2 * p][c:] + av[:, 0:C_DV], wq[2 * p + 1][c:] + av[:, C_DV:]]
        for h in heads:
            state_ref[h] = s_new[h]
            y_ref[rows, hsl[h]] = (_rms(o[h], cn) * zs_ref[rows, hsl[h]].astype(F32)).astype(BF16)
        return carry

    lax.fori_loop(0, GDN_NCH, recur, 0)


def _gdn(uc, sm, a_log, dt_bias, c_norm, bsz, seq):
    c, tt = C_CHUNK, GDN_TT
    nt = seq // tt
    pad = lambda v: jnp.pad(v.astype(F32), (0, LANES - v.shape[0])).reshape(1, LANES)
    li = jnp.arange(LANES)[:, None]
    cj = jnp.arange(2048)[None, :]
    lp = li % 16
    e = ((li < 48) & (((lp < 8) & (cj < 1024) & (cj // LANES == lp))
                      | ((lp >= 8) & (cj >= 1024) & ((cj - 1024) // LANES == lp - 8)))).astype(BF16)
    ti = jnp.arange(tt)
    lbd = ((ti[:, None] // c == ti[None, :] // c) & (ti[:, None] >= ti[None, :])).astype(BF16)
    row = lambda blockcol: (lambda b, t: (b * nt + t, blockcol))
    const = lambda b, t: (0, 0)
    return pl.pallas_call(
        _gdn_kernel,
        grid=(bsz, nt),
        in_specs=[
            pl.BlockSpec((tt, 512), row(0)),
            pl.BlockSpec((tt, 512), row(1)),
            pl.BlockSpec((tt, 1024), row(1)),
            pl.BlockSpec((tt, 1024), row(2)),
            pl.BlockSpec((tt, SM_W), row(0)),
            pl.BlockSpec((1, LANES), const),
            pl.BlockSpec((1, LANES), const),
            pl.BlockSpec((1, LANES), const),
            pl.BlockSpec((LANES, 2048), const),
            pl.BlockSpec((tt, tt), const),
        ],
        out_specs=pl.BlockSpec((tt, 1024), row(0)),
        out_shape=jax.ShapeDtypeStruct((bsz * seq, 1024), BF16),
        scratch_shapes=[
            pltpu.VMEM((C_V_HEADS, C_DK, C_DV), F32),
            pltpu.VMEM((tt, 2048), F32),
            pltpu.VMEM((tt, 1024), F32),
            pltpu.VMEM((tt, 1024), BF16),
            pltpu.VMEM((tt, 1024), BF16),
            pltpu.VMEM((tt, 1024), BF16),
            pltpu.VMEM((tt, 512), BF16),
        ],
        compiler_params=_cparams(("parallel", "arbitrary")),
        name="gdn",
    )(uc, uc, uc, uc, sm, pad(a_log), pad(dt_bias), c_norm.astype(F32).reshape(1, LANES), e, lbd)


def _merge_kernel(x_ref, o1_ref, o2_ref, o3_ref, l1_ref, l2_ref, l3_ref, yb_ref, yc_ref, ug_ref,
                  wa_ref, wb_ref, wc_ref, wo_ref, out_ref, nat_ref):
    tm = x_ref.shape[0]
    for slot, (dil, src) in enumerate(((4, o2_ref), (4, l2_ref), (16, o3_ref), (16, l3_ref))):
        for r in range(dil):
            for g in range(4):
                col = r * 512 + g * LANES
                nat_ref[slot, g, pl.ds(r, tm // dil, stride=dil), :] = src[:, col:col + LANES].astype(F32)
    parts = []
    for g in range(4):
        sl = slice(g * LANES, (g + 1) * LANES)
        l1, l2, l3 = l1_ref[:, sl], nat_ref[1, g], nat_ref[3, g]
        m = jnp.maximum(jnp.maximum(l1, l2), l3)
        e1, e2, e3 = jnp.exp(l1 - m), jnp.exp(l2 - m), jnp.exp(l3 - m)
        parts.append(((e1 * o1_ref[:, sl].astype(F32) + e2 * nat_ref[0, g] + e3 * nat_ref[2, g])
                      / (e1 + e2 + e3)).astype(BF16))
    ya = jnp.concatenate(parts, axis=1)
    merged = (ug_ref[:, 0:1024].astype(F32) * jnp.dot(ya, wa_ref[...], preferred_element_type=F32)
              + ug_ref[:, 1024:2048].astype(F32) * jnp.dot(yb_ref[...], wb_ref[...], preferred_element_type=F32)
              + ug_ref[:, 2048:3072].astype(F32) * jnp.dot(yc_ref[...], wc_ref[...], preferred_element_type=F32))
    out_ref[...] = x_ref[...] + jnp.dot(merged.astype(BF16), wo_ref[...], preferred_element_type=F32)


def _merge(x2, o_list, lse_list, yb, yc, ug, wa, wb, wc, wo, tm):
    m = x2.shape[0]
    row = lambda i: (i, 0)
    const = lambda i: (0, 0)
    rs = lambda w, dil=1: pl.BlockSpec((tm // dil, dil * w), row)
    return pl.pallas_call(
        _merge_kernel,
        grid=(m // tm,),
        in_specs=[rs(D_MODEL), rs(512), rs(512, 4), rs(512, 16), rs(512), rs(512, 4), rs(512, 16),
                  rs(512), rs(1024), rs(UG_W),
                  pl.BlockSpec((512, D_MODEL), const), pl.BlockSpec((512, D_MODEL), const),
                  pl.BlockSpec((1024, D_MODEL), const), pl.BlockSpec((D_MODEL, D_MODEL), const)],
        out_specs=rs(D_MODEL),
        out_shape=jax.ShapeDtypeStruct((m, D_MODEL), F32),
        scratch_shapes=[pltpu.VMEM((4, 4, tm, LANES), F32)],
        compiler_params=_cparams(("parallel",)),
        name="merge",
    )(x2, *o_list, *lse_list, yb, yc, ug, wa, wb, wc, wo)


def _ffn_kernel(x_ref, g_ref, w1_ref, w2_ref, gf_ref, out_ref, *, final_norm):
    x = x_ref[...]
    h = _rms(x, g_ref[...]).astype(BF16)
    acc = x
    for c in range(D_FF // 1024):
        sl = slice(c * 1024, (c + 1) * 1024)
        a = jnp.maximum(jnp.dot(h, w1_ref[:, sl], preferred_element_type=F32), 0.0)
        acc = acc + jnp.dot((a * a).astype(BF16), w2_ref[sl, :], preferred_element_type=F32)
    out_ref[...] = _rms(acc, gf_ref[...]) if final_norm else acc


def _ffn(x2, gain, w1, w2, gain_final, final_norm, tm):
    m = x2.shape[0]
    row = lambda i: (i, 0)
    const = lambda i: (0, 0)
    return pl.pallas_call(
        functools.partial(_ffn_kernel, final_norm=final_norm),
        grid=(m // tm,),
        in_specs=[pl.BlockSpec((tm, D_MODEL), row), pl.BlockSpec((1, D_MODEL), const),
                  pl.BlockSpec((D_MODEL, D_FF), const), pl.BlockSpec((D_FF, D_MODEL), const),
                  pl.BlockSpec((1, D_MODEL), const)],
        out_specs=pl.BlockSpec((tm, D_MODEL), row),
        out_shape=jax.ShapeDtypeStruct((m, D_MODEL), F32),
        compiler_params=_cparams(("parallel",)),
        name="ffn",
    )(x2, gain, w1, w2, gain_final)


def _rope_tables(positions):
    half = ROT_DIM // 2
    inv_freq = jnp.power(ROPE_THETA, -jnp.arange(0, ROT_DIM, 2, dtype=F32) / ROT_DIM)
    ang = positions.astype(F32).reshape(-1, 1) * inv_freq
    cos, sin = jnp.cos(ang), jnp.sin(ang)
    n = ang.shape[0]
    rest = HEAD_DIM - ROT_DIM
    cos64 = jnp.concatenate([cos, cos, jnp.ones((n, rest), F32)], axis=1)
    s1 = jnp.concatenate([-sin, jnp.zeros((n, half + rest), F32)], axis=1)
    s2 = jnp.concatenate([jnp.zeros((n, half), F32), sin, jnp.zeros((n, rest), F32)], axis=1)
    two = lambda t: jnp.concatenate([t, t], axis=1)
    return two(cos64), two(s1), two(s2)


def kernel(x, positions, norm_mix, w_in, b_in, conv_w, a_log, dt_bias, sinks, c_norm, w_branch_a, w_branch_b,
           w_branch_c, w_out, norm_ffn, w_ff1, w_ff2, norm_final):
    bsz, seq, d = x.shape
    depth = w_in.shape[0]
    assert d == D_MODEL and seq % (16 * BLK) == 0
    m = bsz * seq
    cosv, s1, s2 = _rope_tables(positions)
    x2 = x.reshape(m, d)
    ca = 5376
    for layer in range(depth):
        wl, bl = w_in[layer], b_in[layer]
        w_all = jnp.concatenate([wl[:, :ca], wl[:, ca + 16:], wl[:, ca:ca + 16],
                                 jnp.zeros((d, SM_W - 16), F32)], axis=1).astype(BF16)
        b_all = jnp.concatenate([bl[:ca], bl[ca + 16:], bl[ca:ca + 16], jnp.zeros((SM_W - 16,), F32)]).reshape(1, -1)
        ua, ua4, ua16, ub, uc, ug, sm = _inproj(x2, norm_mix[layer].reshape(1, d), w_all, b_all, cosv, s1, s2,
                                                conv_w[layer].astype(F32), seq, tm=256)
        o_list, lse_list = [], []
        for (window, dil), ua_d in zip(A_CONFIGS, (ua, ua4, ua16)):
            o, lse = _attn_a(ua_d, bsz, seq, window, dil)
            o_list.append(o)
            lse_list.append(lse)
        yb = _attn_b(ub, sinks[layer].astype(F32), bsz, seq)
        yc = _gdn(uc, sm, a_log[layer], dt_bias[layer], c_norm[layer], bsz, seq)
        x2 = _merge(x2, o_list, lse_list, yb, yc, ug, w_branch_a[layer].astype(BF16), w_branch_b[layer].astype(BF16),
                    w_branch_c[layer].astype(BF16), w_out[layer].astype(BF16), tm=512)
        x2 = _ffn(x2, norm_ffn[layer].reshape(1, d), w_ff1[layer].astype(BF16), w_ff2[layer].astype(BF16),
                  norm_final.reshape(1, d), final_norm=(layer == depth - 1), tm=512)
    return x2.reshape(bsz, seq, d)
```

```python
import functools
import math

import jax
import jax.numpy as jnp
from jax import lax
from jax.experimental import pallas as pl
from jax.experimental.pallas import tpu as pltpu

F32 = jnp.float32
BF16 = jnp.bfloat16

D_MODEL = 1024
HEAD_DIM = 64
ROT_DIM = 16
ROPE_THETA = 500000.0
BLK = 128
NEG_INF = -1e30
EPS = 1e-6
A_CONFIGS = ((128, 1), (512, 4), (2048, 16))
B_WINDOW = 128
C_V_HEADS = 8
C_DK = 128
C_DV = 128
C_CONV = 4
C_CHUNK = 64
D_FF = 4096

LANES = 128
UA_W = 1536
UB_W = 768
UC_W = 3072
UG_W = 3072
SM_W = 128
W_ALL = UA_W + UB_W + UC_W + UG_W + SM_W
VMEM_LIMIT = 56 * 1024 * 1024


def _cparams(sem):
    return pltpu.CompilerParams(dimension_semantics=sem, vmem_limit_bytes=VMEM_LIMIT)


def _rms(x, gain):
    return x * lax.rsqrt(jnp.mean(x * x, axis=-1, keepdims=True) + EPS) * gain


def _inproj_kernel(x_ref, g_ref, wm_ref, wg_ref, ws_ref, b_ref, cos_ref, s1_ref, s2_ref, cw_ref,
                   ua_ref, ua4_ref, ua16_ref, ub_ref, uc_ref, ug_ref, sm_ref, stage_ref, ext_ref, h_ref,
                   *, tiles_per_seq):
    h_ref[...] = _rms(x_ref[...], g_ref[...]).astype(BF16)
    cosv, s1, s2 = cos_ref[...], s1_ref[...], s2_ref[...]
    w_main = UA_W + UB_W + UC_W

    def seg(off, width):
        if off < w_main:
            w = wm_ref[:, off:off + width]
        elif off < w_main + UG_W:
            w = wg_ref[:, off - w_main:off - w_main + width]
        else:
            w = ws_ref[...]
        return jnp.dot(h_ref[...], w, preferred_element_type=F32) + b_ref[:, off:off + width]

    def rope(a, scale):
        parts = []
        for g in range(a.shape[1] // LANES):
            t = a[:, g * LANES:(g + 1) * LANES]
            r = t * cosv + pltpu.roll(t, LANES - 8, 1) * s1 + pltpu.roll(t, 8, 1) * s2
            parts.append(r * scale if scale != 1.0 else r)
        return jnp.concatenate(parts, axis=1)

    qscale = HEAD_DIM ** -0.5
    tm = x_ref.shape[0]

    def seg_a(c):
        val = seg(c * 512, 512)
        if c < 2:
            val = rope(val, qscale if c == 0 else 1.0)
        ua_ref[:, c * 512:(c + 1) * 512] = val.astype(BF16)
        for gl in range(4):
            g = 4 * c + gl
            stage_ref[g] = val[:, gl * LANES:(gl + 1) * LANES]
            for dil, dst in ((4, ua4_ref), (16, ua16_ref)):
                for r in range(dil):
                    col = r * UA_W + g * LANES
                    dst[:, col:col + LANES] = stage_ref[g, pl.ds(r, tm // dil, stride=dil), :].astype(BF16)

    def seg_b():
        ub_ref[:, 0:512] = rope(seg(UA_W, 512), qscale).astype(BF16)
        ub_ref[:, 512:640] = rope(seg(UA_W + 512, 128), 1.0).astype(BF16)
        ub_ref[:, 640:768] = seg(UA_W + 640, 128).astype(BF16)

    def silu(t):
        return t * (0.5 + 0.5 * jnp.tanh(0.5 * t))

    @pl.when(pl.program_id(0) % tiles_per_seq == 0)
    def _():
        ext_ref[:, 0:8, :] = jnp.zeros((ext_ref.shape[0], 8, LANES), F32)

    def seg_conv(c):
        val = seg(UA_W + UB_W + c * 512, 512)
        for gl in range(4):
            g = 4 * c + gl
            sl = slice(g * LANES, (g + 1) * LANES)
            cur = val[:, gl * LANES:(gl + 1) * LANES]
            ext_ref[g, 8:8 + tm, :] = cur
            y = cur * cw_ref[C_CONV - 1:C_CONV, sl]
            for j in range(C_CONV - 1):
                y = y + ext_ref[g, 5 + j:5 + j + tm, :] * cw_ref[j:j + 1, sl]
            ext_ref[g, 0:8, :] = cur[tm - 8:tm]
            y = silu(y)
            if c < 2:
                scale = C_DK ** -0.5 if c == 0 else 1.0
                y = y * (lax.rsqrt(jnp.sum(y * y, axis=-1, keepdims=True) + EPS) * scale)
            uc_ref[:, sl] = y.astype(BF16)

    def seg_z(c):
        uc_ref[:, c * 512:(c + 1) * 512] = silu(seg(UA_W + UB_W + c * 512, 512)).astype(BF16)

    def seg_gate(c):
        ug_ref[:, c * 256:(c + 1) * 256] = jax.nn.sigmoid(seg(UA_W + UB_W + UC_W + c * 256, 256)).astype(BF16)

    def seg_small():
        sm_ref[...] = seg(UA_W + UB_W + UC_W + UG_W, SM_W)

    heavy = ([functools.partial(seg_a, c) for c in range(3)] + [seg_b]
             + [functools.partial(seg_conv, c) for c in range(4)]
             + [functools.partial(seg_z, c) for c in (4, 5)] + [seg_small])
    gates = [functools.partial(seg_gate, c) for c in range(12)]
    for k, fn in enumerate(heavy):
        fn()
        for gfn in gates[k * len(gates) // len(heavy):(k + 1) * len(gates) // len(heavy)]:
            gfn()


def _inproj(x2, gain, w_main, w_gate, w_small, b_all, cosv, s1, s2, conv_w, seq, tm):
    m = x2.shape[0]
    row = lambda i: (i, 0)
    const = lambda i: (0, 0)
    return pl.pallas_call(
        functools.partial(_inproj_kernel, tiles_per_seq=seq // tm),
        grid=(m // tm,),
        in_specs=[
            pl.BlockSpec((tm, D_MODEL), row),
            pl.BlockSpec((1, D_MODEL), const),
            pl.BlockSpec((D_MODEL, UA_W + UB_W + UC_W), const),
            pl.BlockSpec((D_MODEL, UG_W), const),
            pl.BlockSpec((D_MODEL, SM_W), const),
            pl.BlockSpec((1, W_ALL), const),
            pl.BlockSpec((tm, LANES), row),
            pl.BlockSpec((tm, LANES), row),
            pl.BlockSpec((tm, LANES), row),
            pl.BlockSpec((C_CONV, 2048), const),
        ],
        out_specs=[
            pl.BlockSpec((tm, UA_W), row),
            pl.BlockSpec((tm // 4, 4 * UA_W), row),
            pl.BlockSpec((tm // 16, 16 * UA_W), row),
            pl.BlockSpec((tm, UB_W), row),
            pl.BlockSpec((tm, UC_W), row),
            pl.BlockSpec((tm, UG_W), row),
            pl.BlockSpec((tm, SM_W), row),
        ],
        out_shape=[
            jax.ShapeDtypeStruct((m, UA_W), BF16),
            jax.ShapeDtypeStruct((m // 4, 4 * UA_W), BF16),
            jax.ShapeDtypeStruct((m // 16, 16 * UA_W), BF16),
            jax.ShapeDtypeStruct((m, UB_W), BF16),
            jax.ShapeDtypeStruct((m, UC_W), BF16),
            jax.ShapeDtypeStruct((m, UG_W), BF16),
            jax.ShapeDtypeStruct((m, SM_W), F32),
        ],
        scratch_shapes=[pltpu.VMEM((UA_W // LANES, tm, LANES), F32),
                        pltpu.VMEM((2048 // LANES, tm + 8, LANES), F32),
                        pltpu.VMEM((tm, D_MODEL), BF16)],
        compiler_params=_cparams(("arbitrary",)),
        name="inproj",
    )(x2, gain, w_main, w_gate, w_small, b_all, cosv, s1, s2, conv_w)


ATT_QT = 512


def _band_valid(max_dist, first_block):
    qi = lax.broadcasted_iota(jnp.int32, (BLK, 2 * BLK), 0)
    kj = lax.broadcasted_iota(jnp.int32, (BLK, 2 * BLK), 1)
    dist = BLK + qi - kj
    band = (dist >= 0) & (dist <= max_dist)
    if first_block is None:
        return band
    return band & (jnp.logical_not(first_block) | (kj >= BLK))


def _kv_window(cur_ref, prev_ref, i, sl):
    if i == 0:
        return jnp.concatenate([prev_ref[0, :, sl], cur_ref[0, 0:BLK, sl]], axis=0)
    return cur_ref[0, (i - 1) * BLK:(i + 1) * BLK, sl]


ATT_LOOKAHEAD = 2


def _attention_tile(nblk, get_q, get_k, get_v, get_valid, get_sink, emit):
    lane = lax.broadcasted_iota(jnp.int32, (BLK, LANES), 1)
    first = lane < HEAD_DIM
    units = [(i, p, hh) for i in range(nblk) for p in range(4) for hh in range(2)]

    def scores(u):
        i, p, hh = u
        qp = get_q(i, p)
        qm = jnp.where(first if hh == 0 else ~first, qp, jnp.zeros_like(qp))
        return lax.dot_general(qm, get_k(i, p), (((1,), (1,)), ((), ())), preferred_element_type=F32)

    def finish(u, s):
        i, p, hh = u
        sink = get_sink(p, hh)
        s = jnp.where(get_valid(i), s, NEG_INF)
        m = jnp.max(s, axis=-1, keepdims=True)
        if sink is not None:
            m = jnp.maximum(m, sink)
        pr = jnp.exp(s - m)
        den = jnp.sum(pr, axis=-1, keepdims=True)
        if sink is not None:
            den = den + jnp.exp(sink - m)
        return jnp.dot(pr.astype(BF16), get_v(i, p), preferred_element_type=F32), m, den

    pending = {k: scores(units[k]) for k in range(min(ATT_LOOKAHEAD, len(units)))}
    done = {}
    for k, u in enumerate(units):
        if k + ATT_LOOKAHEAD < len(units):
            pending[k + ATT_LOOKAHEAD] = scores(units[k + ATT_LOOKAHEAD])
        done[u] = finish(u, pending.pop(k))
        i, p, hh = u
        if hh == 1:
            (o0, m0, d0), (o1, m1, d1) = done.pop((i, p, 0)), done.pop((i, p, 1))
            den = jnp.where(first, d0, d1)
            emit(i, p, jnp.where(first, o0, o1) / den, jnp.where(first, m0, m1) + jnp.log(den))


def _attn_a_kernel(q_ref, kc_ref, kp_ref, vc_ref, vp_ref, o_ref, lse_ref, *, max_dist):
    first_tile = pl.program_id(2) == 0
    lanes = lambda p: slice(p * LANES, (p + 1) * LANES)
    rows = lambda i: slice(i * BLK, (i + 1) * BLK)
    valid = {}

    def get_valid(i):
        key = min(i, 1)
        if key not in valid:
            valid[key] = _band_valid(max_dist, first_tile if i == 0 else None)
        return valid[key]

    def emit(i, p, o, lse):
        o_ref[0, rows(i), lanes(p)] = o.astype(BF16)
        lse_ref[0, rows(i), lanes(p)] = lse

    _attention_tile(q_ref.shape[1] // BLK,
                    lambda i, p: q_ref[0, rows(i), lanes(p)],
                    lambda i, p: _kv_window(kc_ref, kp_ref, i, lanes(p)),
                    lambda i, p: _kv_window(vc_ref, vp_ref, i, lanes(p)),
                    get_valid, lambda p, hh: None, emit)


def _attn_a(ua_d, bsz, seq, window, dil):
    l = seq // dil
    steps = window // dil
    view = ua_d.reshape(bsz, l, dil * UA_W)
    qt = min(ATT_QT, l)
    blk = (1, qt, 512)
    pblk = (1, BLK, 512)
    cur = lambda c: (lambda b, r, j: (b, j, 3 * r + c))
    prev = lambda c: (lambda b, r, j: (b, jnp.maximum(j * (qt // BLK) - 1, 0), 3 * r + c))
    out = lambda b, r, j: (b, j, r)
    o, lse = pl.pallas_call(
        functools.partial(_attn_a_kernel, max_dist=steps),
        grid=(bsz, dil, l // qt),
        in_specs=[pl.BlockSpec(blk, cur(0)), pl.BlockSpec(blk, cur(1)), pl.BlockSpec(pblk, prev(1)),
                  pl.BlockSpec(blk, cur(2)), pl.BlockSpec(pblk, prev(2))],
        out_specs=[pl.BlockSpec(blk, out), pl.BlockSpec(blk, out)],
        out_shape=[jax.ShapeDtypeStruct((bsz, l, dil * 512), BF16),
                   jax.ShapeDtypeStruct((bsz, l, dil * 512), F32)],
        compiler_params=_cparams(("parallel", "parallel", "arbitrary")),
        name=f"attn_a_d{dil}",
    )(view, view, view, view, view)
    return o.reshape(bsz * l, dil * 512), lse.reshape(bsz * l, dil * 512)


def _attn_b_kernel(sink_ref, q_ref, kc_ref, kp_ref, vc_ref, vp_ref, o_ref):
    first_tile = pl.program_id(1) == 0
    lane = lax.broadcasted_iota(jnp.int32, (2 * BLK, LANES), 1)
    first = lane < HEAD_DIM
    lanes = lambda p: slice(p * LANES, (p + 1) * LANES)
    rows = lambda i: slice(i * BLK, (i + 1) * BLK)
    valid, dup = {}, {}

    def get_valid(i):
        key = min(i, 1)
        if key not in valid:
            valid[key] = _band_valid(B_WINDOW - 1, first_tile if i == 0 else None)
        return valid[key]

    def get_dup(cur_ref, prev_ref, i, kv):
        key = (id(cur_ref), i)
        if key not in dup:
            t = _kv_window(cur_ref, prev_ref, i, slice(0, LANES))
            t_sw = pltpu.roll(t, HEAD_DIM, 1)
            dup[key] = (jnp.where(first, t, t_sw), jnp.where(first, t_sw, t))
        return dup[key][kv]

    def emit(i, p, o, lse):
        o_ref[0, rows(i), lanes(p)] = o.astype(BF16)

    _attention_tile(q_ref.shape[1] // BLK,
                    lambda i, p: q_ref[0, rows(i), lanes(p)],
                    lambda i, p: get_dup(kc_ref, kp_ref, i, p // 2),
                    lambda i, p: get_dup(vc_ref, vp_ref, i, p // 2),
                    get_valid, lambda p, hh: sink_ref[2 * p + hh], emit)


def _attn_b(ub, sinks, bsz, seq):
    view = ub.reshape(bsz, seq, UB_W)
    qt = ATT_QT
    cur = lambda c: (lambda b, j: (b, j, c))
    prev = lambda c: (lambda b, j: (b, jnp.maximum(j * (qt // BLK) - 1, 0), c))
    o = pl.pallas_call(
        _attn_b_kernel,
        grid=(bsz, seq // qt),
        in_specs=[pl.BlockSpec(memory_space=pltpu.SMEM),
                  pl.BlockSpec((1, qt, 512), lambda b, j: (b, j, 0)),
                  pl.BlockSpec((1, qt, LANES), cur(4)), pl.BlockSpec((1, BLK, LANES), prev(4)),
                  pl.BlockSpec((1, qt, LANES), cur(5)), pl.BlockSpec((1, BLK, LANES), prev(5))],
        out_specs=pl.BlockSpec((1, qt, 512), lambda b, j: (b, j, 0)),
        out_shape=jax.ShapeDtypeStruct((bsz, seq, 512), BF16),
        compiler_params=_cparams(("parallel", "arbitrary")),
        name="attn_b",
    )(sinks, view, view, view, view, view)
    return o.reshape(bsz * seq, 512)


def _split3(a):
    hi = a.astype(BF16)
    r = a - hi.astype(F32)
    mid = r.astype(BF16)
    lo = (r - mid.astype(F32)).astype(BF16)
    return hi, mid, lo


def _dot_nt(a, b):
    return lax.dot_general(a, b, (((1,), (1,)), ((), ())), preferred_element_type=F32)


def _dot_tn(a, b):
    return lax.dot_general(a, b, (((0,), (0,)), ((), ())), preferred_element_type=F32)


def _mm(a, b):
    return jnp.dot(a.astype(BF16), b.astype(BF16), preferred_element_type=F32)


GDN_TT = 256
GDN_NCH = GDN_TT // C_CHUNK
GDN_CPI = 2
GDN_NBAT = 4


def _gdn_kernel(cq_ref, ck_ref, cv_ref, zs_ref, sm_ref, alog_ref, dtb_ref, cn_ref, e_ref, lbd_ref,
                y_ref, state_ref, expd_ref, u_ref, w_ref, qd_ref, kd_ref, attn_ref):
    c, tt = C_CHUNK, GDN_TT

    @pl.when(pl.program_id(1) == 0)
    def _():
        state_ref[...] = jnp.zeros_like(state_ref)

    nbat = sm_ref.shape[0]
    lane = lax.broadcasted_iota(jnp.int32, (tt, LANES), 1)
    lbd = lbd_ref[...]
    for nb in range(nbat):
        sm = sm_ref[nb]
        z = sm + dtb_ref[...]
        softplus = jnp.maximum(z, 0.0) + jnp.log1p(jnp.exp(-jnp.abs(z)))
        g = -jnp.exp(alog_ref[...]) * softplus
        beta = jax.nn.sigmoid(sm)
        g_hi, g_mid, g_lo = _split3(g)
        gc = (jnp.dot(lbd, g_hi, preferred_element_type=F32) + jnp.dot(lbd, g_mid, preferred_element_type=F32)
              + jnp.dot(lbd, g_lo, preferred_element_type=F32))
        c_hi, c_mid, c_lo = _split3(jnp.where(lane < C_V_HEADS, gc, beta))
        pieces = jnp.where(lane < 16, c_hi,
                           jnp.where(lane < 32, pltpu.roll(c_mid, 16, 1),
                                     jnp.where(lane < 48, pltpu.roll(c_lo, 32, 1), jnp.zeros_like(c_hi))))
        for s in range(4):
            sl = slice(s * 512, (s + 1) * 512)
            expd_ref[nb, :, sl] = jnp.dot(pieces, e_ref[:, sl], preferred_element_type=F32)

    ri = lax.broadcasted_iota(jnp.int32, (c, LANES), 0)
    lj = lax.broadcasted_iota(jnp.int32, (c, LANES), 1)
    cj = lj & (c - 1)
    left = lj < c
    tril = ri >= cj
    strict = ri > cj
    eye = (ri == cj).astype(F32)
    bi = lax.broadcasted_iota(jnp.int32, (2 * c, LANES), 0)
    bj = lax.broadcasted_iota(jnp.int32, (2 * c, LANES), 1)
    blockdiag = (bi < c) == (bj < c)

    def bd(m2):
        m16 = m2.astype(BF16)
        return jnp.where(blockdiag, jnp.concatenate([m16, m16], axis=0), jnp.zeros((2 * c, LANES), BF16))

    def chunk_local(i, carry):
        units = []
        for nb, j in [(nb, j) for nb in range(nbat) for j in range(cpi)]:
            r0 = pl.multiple_of((i * cpi + j) * c, c)
            rows = pl.ds(r0, c)
            for p in range(C_V_HEADS // 2):
                hs = [slice(h * LANES, (h + 1) * LANES) for h in (2 * p, 2 * p + 1)]
                ps = slice(p * C_DK, (p + 1) * C_DK)
                units.append(dict(nb=nb, rows=rows, p=p, hs=hs, q16=cq_ref[nb, rows, ps], k16=ck_ref[nb, rows, ps],
                                  v=[cv_ref[nb, rows, s].astype(F32) for s in hs],
                                  ge=[expd_ref[nb, rows, s] for s in hs],
                                  be=[expd_ref[nb, rows, 1024 + s.start:1024 + s.stop] for s in hs]))
        for un in units:
            q16, k16 = un["q16"], un["k16"]
            un["qn"], un["kn"] = q16.astype(F32), k16.astype(F32)
            un["kq"] = _dot_nt(jnp.concatenate([k16, q16], axis=0),
                               jnp.concatenate([k16, k16], axis=0))
        for un in units:
            ge, be, kq = un["ge"], un["be"], un["kq"]
            grow = jnp.concatenate(ge, axis=0).T[0:c, :]
            diff = jnp.where(left, ge[0], ge[1]) - grow
            dec = jnp.where(tril, jnp.exp(jnp.where(tril, diff, 0.0)), 0.0)
            un["pw"] = jnp.where(strict, kq[0:c] * dec * jnp.where(left, be[0], be[1]), 0.0) * -1.0
            un["attn"] = jnp.where(tril, kq[c:] * dec, 0.0).astype(BF16)
            un["x"] = eye + un["pw"]
        for un in units:
            un["pw"] = _mm(un["pw"], bd(un["pw"]))
        for _ in range(4):
            for un in units:
                r = _mm(jnp.concatenate([un["pw"], un["x"]], axis=0), bd(un["pw"]))
                un["x"] = un["x"] + r[c:]
                un["pw"] = r[0:c]
        for un in units:
            un["x"] = un["x"] + _mm(un["x"], bd(un["pw"]))
        zero = jnp.zeros((c, 2 * C_DV), F32)
        for un in units:
            rhs, un["qd"], un["kd"] = [], [], []
            for ge, be, v in zip(un["ge"], un["be"], un["v"]):
                eg = jnp.exp(ge)
                rhs.append(jnp.concatenate([v * be, un["kn"] * be * eg], axis=1))
                un["qd"].append((un["qn"] * eg).astype(BF16))
                un["kd"].append((un["kn"] * jnp.exp(ge[c - 1:c, :] - ge)).astype(BF16))
            bdr = jnp.concatenate([jnp.concatenate([rhs[0], zero], axis=1),
                                   jnp.concatenate([zero, rhs[1]], axis=1)], axis=0)
            un["uw"] = _mm(un["x"], bdr)
        for un in units:
            nb, rows, p = un["nb"], un["rows"], un["p"]
            for t, s in enumerate(un["hs"]):
                u_ref[nb, rows, s] = un["uw"][:, 2 * t * C_DV:(2 * t + 1) * C_DV]
                w_ref[nb, rows, s] = un["uw"][:, (2 * t + 1) * C_DV:(2 * t + 2) * C_DV].astype(BF16)
                qd_ref[nb, rows, s] = un["qd"][t]
                kd_ref[nb, rows, s] = un["kd"][t]
            attn_ref[nb, rows, p * LANES:(p + 1) * LANES] = un["attn"]
        return carry

    cpi = max(1, GDN_CPI // nbat)
    lax.fori_loop(0, GDN_NCH // cpi, chunk_local, 0)
    cn = cn_ref[...]

    def recur(i, carry):
        r0 = pl.multiple_of(i * c, c)
        rows = pl.ds(r0, c)
        last8 = pl.ds(pl.multiple_of(r0 + c - 8, 8), 8)
        hsl = [slice(h * C_DV, (h + 1) * C_DV) for h in range(C_V_HEADS)]
        chains = [(nb, h) for nb in range(nbat) for h in range(C_V_HEADS)]
        s = {k: state_ref[k[0] * C_V_HEADS + k[1]] for k in chains}
        wq = {(nb, h): jnp.dot(jnp.concatenate([w_ref[nb, rows, hsl[h]], qd_ref[nb, rows, hsl[h]]], axis=0),
                               s[nb, h].astype(BF16), preferred_element_type=F32) for nb, h in chains}
        v16 = {(nb, h): (u_ref[nb, rows, hsl[h]] - wq[nb, h][0:c]).astype(BF16) for nb, h in chains}
        s_new = {(nb, h): s[nb, h] * jnp.exp(expd_ref[nb, last8, hsl[h]][7:8]) + _dot_tn(kd_ref[nb, rows, hsl[h]], v16[nb, h])
                 for nb, h in chains}
        zero = jnp.zeros((c, C_DV), BF16)
        o = {}
        for nb in range(nbat):
            for p in range(C_V_HEADS // 2):
                bdv = jnp.concatenate([jnp.concatenate([v16[nb, 2 * p], zero], axis=1),
                                       jnp.concatenate([zero, v16[nb, 2 * p + 1]], axis=1)], axis=0)
                av = jnp.dot(attn_ref[nb, rows, p * LANES:(p + 1) * LANES], bdv, preferred_element_type=F32)
                o[nb, 2 * p] = wq[nb, 2 * p][c:] + av[:, 0:C_DV]
                o[nb, 2 * p + 1] = wq[nb, 2 * p + 1][c:] + av[:, C_DV:]
        for nb, h in chains:
            state_ref[nb * C_V_HEADS + h] = s_new[nb, h]
            y_ref[nb, rows, hsl[h]] = (_rms(o[nb, h], cn) * zs_ref[nb, rows, hsl[h]].astype(F32)).astype(BF16)
        return carry

    lax.fori_loop(0, GDN_NCH, recur, 0)


def _gdn(uc, sm, a_log, dt_bias, c_norm, bsz, seq):
    c, tt = C_CHUNK, GDN_TT
    nt = seq // tt
    pad = lambda v: jnp.pad(v.astype(F32), (0, LANES - v.shape[0])).reshape(1, LANES)
    li = jnp.arange(LANES)[:, None]
    cj = jnp.arange(2048)[None, :]
    lp = li % 16
    e = ((li < 48) & (((lp < 8) & (cj < 1024) & (cj // LANES == lp))
                      | ((lp >= 8) & (cj >= 1024) & ((cj - 1024) // LANES == lp - 8)))).astype(BF16)
    ti = jnp.arange(tt)
    lbd = ((ti[:, None] // c == ti[None, :] // c) & (ti[:, None] >= ti[None, :])).astype(BF16)
    nbat = GDN_NBAT if bsz % GDN_NBAT == 0 else 1
    uc3 = uc.reshape(bsz, seq, UC_W)
    sm3 = sm.reshape(bsz, seq, SM_W)
    row = lambda blockcol: (lambda b, t: (b, t, blockcol))
    const = lambda b, t: (0, 0)
    yc = pl.pallas_call(
        _gdn_kernel,
        grid=(bsz // nbat, nt),
        in_specs=[
            pl.BlockSpec((nbat, tt, 512), row(0)),
            pl.BlockSpec((nbat, tt, 512), row(1)),
            pl.BlockSpec((nbat, tt, 1024), row(1)),
            pl.BlockSpec((nbat, tt, 1024), row(2)),
            pl.BlockSpec((nbat, tt, SM_W), row(0)),
            pl.BlockSpec((1, LANES), const),
            pl.BlockSpec((1, LANES), const),
            pl.BlockSpec((1, LANES), const),
            pl.BlockSpec((LANES, 2048), const),
            pl.BlockSpec((tt, tt), const),
        ],
        out_specs=pl.BlockSpec((nbat, tt, 1024), row(0)),
        out_shape=jax.ShapeDtypeStruct((bsz, seq, 1024), BF16),
        scratch_shapes=[
            pltpu.VMEM((nbat * C_V_HEADS, C_DK, C_DV), F32),
            pltpu.VMEM((nbat, tt, 2048), F32),
            pltpu.VMEM((nbat, tt, 1024), F32),
            pltpu.VMEM((nbat, tt, 1024), BF16),
            pltpu.VMEM((nbat, tt, 1024), BF16),
            pltpu.VMEM((nbat, tt, 1024), BF16),
            pltpu.VMEM((nbat, tt, 512), BF16),
        ],
        compiler_params=_cparams(("parallel", "arbitrary")),
        name="gdn",
    )(uc3, uc3, uc3, uc3, sm3, pad(a_log), pad(dt_bias), c_norm.astype(F32).reshape(1, LANES), e, lbd)
    return yc.reshape(bsz * seq, 1024)


def _merge_kernel(x_ref, o1_ref, o2_ref, o3_ref, l1_ref, l2_ref, l3_ref, yb_ref, yc_ref, ug_ref,
                  wa_ref, wb_ref, wc_ref, wo_ref, out_ref, nat_ref):
    tm = x_ref.shape[0]
    for slot, (dil, src) in enumerate(((4, o2_ref), (4, l2_ref), (16, o3_ref), (16, l3_ref))):
        for r in range(dil):
            for g in range(4):
                col = r * 512 + g * LANES
                nat_ref[slot, g, pl.ds(r, tm // dil, stride=dil), :] = src[:, col:col + LANES].astype(F32)
    parts = []
    for g in range(4):
        sl = slice(g * LANES, (g + 1) * LANES)
        l1, l2, l3 = l1_ref[:, sl], nat_ref[1, g], nat_ref[3, g]
        m = jnp.maximum(jnp.maximum(l1, l2), l3)
        e1, e2, e3 = jnp.exp(l1 - m), jnp.exp(l2 - m), jnp.exp(l3 - m)
        parts.append(((e1 * o1_ref[:, sl].astype(F32) + e2 * nat_ref[0, g] + e3 * nat_ref[2, g])
                      / (e1 + e2 + e3)).astype(BF16))
    ya = jnp.concatenate(parts, axis=1)
    merged = (ug_ref[:, 0:1024].astype(F32) * jnp.dot(ya, wa_ref[...], preferred_element_type=F32)
              + ug_ref[:, 1024:2048].astype(F32) * jnp.dot(yb_ref[...], wb_ref[...], preferred_element_type=F32)
              + ug_ref[:, 2048:3072].astype(F32) * jnp.dot(yc_ref[...], wc_ref[...], preferred_element_type=F32))
    out_ref[...] = x_ref[...] + jnp.dot(merged.astype(BF16), wo_ref[...], preferred_element_type=F32)


def _merge(x2, o_list, lse_list, yb, yc, ug, wa, wb, wc, wo, tm):
    m = x2.shape[0]
    row = lambda i: (i, 0)
    const = lambda i: (0, 0)
    rs = lambda w, dil=1: pl.BlockSpec((tm // dil, dil * w), row)
    return pl.pallas_call(
        _merge_kernel,
        grid=(m // tm,),
        in_specs=[rs(D_MODEL), rs(512), rs(512, 4), rs(512, 16), rs(512), rs(512, 4), rs(512, 16),
                  rs(512), rs(1024), rs(UG_W),
                  pl.BlockSpec((512, D_MODEL), const), pl.BlockSpec((512, D_MODEL), const),
                  pl.BlockSpec((1024, D_MODEL), const), pl.BlockSpec((D_MODEL, D_MODEL), const)],
        out_specs=rs(D_MODEL),
        out_shape=jax.ShapeDtypeStruct((m, D_MODEL), F32),
        scratch_shapes=[pltpu.VMEM((4, 4, tm, LANES), F32)],
        compiler_params=_cparams(("parallel",)),
        name="merge",
    )(x2, *o_list, *lse_list, yb, yc, ug, wa, wb, wc, wo)


def _ffn_kernel(x_ref, g_ref, w1_ref, w2_ref, gf_ref, out_ref, *, final_norm):
    x = x_ref[...]
    h = _rms(x, g_ref[...]).astype(BF16)
    acc = x
    for c in range(D_FF // 1024):
        sl = slice(c * 1024, (c + 1) * 1024)
        a = jnp.maximum(jnp.dot(h, w1_ref[:, sl], preferred_element_type=F32), 0.0)
        acc = acc + jnp.dot((a * a).astype(BF16), w2_ref[sl, :], preferred_element_type=F32)
    out_ref[...] = _rms(acc, gf_ref[...]) if final_norm else acc


def _ffn(x2, gain, w1, w2, gain_final, final_norm, tm):
    m = x2.shape[0]
    row = lambda i: (i, 0)
    const = lambda i: (0, 0)
    return pl.pallas_call(
        functools.partial(_ffn_kernel, final_norm=final_norm),
        grid=(m // tm,),
        in_specs=[pl.BlockSpec((tm, D_MODEL), row), pl.BlockSpec((1, D_MODEL), const),
                  pl.BlockSpec((D_MODEL, D_FF), const), pl.BlockSpec((D_FF, D_MODEL), const),
                  pl.BlockSpec((1, D_MODEL), const)],
        out_specs=pl.BlockSpec((tm, D_MODEL), row),
        out_shape=jax.ShapeDtypeStruct((m, D_MODEL), F32),
        compiler_params=_cparams(("parallel",)),
        name="ffn",
    )(x2, gain, w1, w2, gain_final)


def _rope_tables(positions):
    half = ROT_DIM // 2
    inv_freq = jnp.power(ROPE_THETA, -jnp.arange(0, ROT_DIM, 2, dtype=F32) / ROT_DIM)
    ang = positions.astype(F32).reshape(-1, 1) * inv_freq
    cos, sin = jnp.cos(ang), jnp.sin(ang)
    n = ang.shape[0]
    rest = HEAD_DIM - ROT_DIM
    cos64 = jnp.concatenate([cos, cos, jnp.ones((n, rest), F32)], axis=1)
    s1 = jnp.concatenate([-sin, jnp.zeros((n, half + rest), F32)], axis=1)
    s2 = jnp.concatenate([jnp.zeros((n, half), F32), sin, jnp.zeros((n, rest), F32)], axis=1)
    two = lambda t: jnp.concatenate([t, t], axis=1)
    return two(cos64), two(s1), two(s2)


def kernel(x, positions, norm_mix, w_in, b_in, conv_w, a_log, dt_bias, sinks, c_norm, w_branch_a, w_branch_b,
           w_branch_c, w_out, norm_ffn, w_ff1, w_ff2, norm_final):
    bsz, seq, d = x.shape
    depth = w_in.shape[0]
    assert d == D_MODEL and seq % (16 * BLK) == 0
    m = bsz * seq
    cosv, s1, s2 = _rope_tables(positions)
    x2 = x.reshape(m, d)
    ca = 5376
    for layer in range(depth):
        wl, bl = w_in[layer], b_in[layer]
        w_main = wl[:, :ca].astype(BF16)
        w_gate = wl[:, ca + 16:].astype(BF16)
        w_small = jnp.pad(wl[:, ca:ca + 16], ((0, 0), (0, SM_W - 16))).astype(BF16)
        b_all = jnp.concatenate([bl[:ca], bl[ca + 16:], bl[ca:ca + 16], jnp.zeros((SM_W - 16,), F32)]).reshape(1, -1)
        ua, ua4, ua16, ub, uc, ug, sm = _inproj(x2, norm_mix[layer].reshape(1, d), w_main, w_gate, w_small, b_all,
                                                cosv, s1, s2, conv_w[layer].astype(F32), seq, tm=256)
        o_list, lse_list = [], []
        for (window, dil), ua_d in zip(A_CONFIGS, (ua, ua4, ua16)):
            o, lse = _attn_a(ua_d, bsz, seq, window, dil)
            o_list.append(o)
            lse_list.append(lse)
        yb = _attn_b(ub, sinks[layer].astype(F32), bsz, seq)
        yc = _gdn(uc, sm, a_log[layer], dt_bias[layer], c_norm[layer], bsz, seq)
        x2 = _merge(x2, o_list, lse_list, yb, yc, ug, w_branch_a[layer].astype(BF16), w_branch_b[layer].astype(BF16),
                    w_branch_c[layer].astype(BF16), w_out[layer].astype(BF16), tm=512)
        x2 = _ffn(x2, norm_ffn[layer].reshape(1, d), w_ff1[layer].astype(BF16), w_ff2[layer].astype(BF16),
                  norm_final.reshape(1, d), final_norm=(layer == depth - 1), tm=512)
    return x2.reshape(bsz, seq, d)
```

```python
import functools
import math

import jax
import jax.numpy as jnp
import numpy as np
from jax import lax
from jax.experimental import pallas as pl
from jax.experimental.pallas import tpu as pltpu

F32 = jnp.float32
BF16 = jnp.bfloat16

D_MODEL = 1024
HEAD_DIM = 64
ROT_DIM = 16
ROPE_THETA = 500000.0
BLK = 128
NEG_INF = -1e30
EPS = 1e-6
A_CONFIGS = ((128, 1), (512, 4), (2048, 16))
B_WINDOW = 128
C_V_HEADS = 8
C_DK = 128
C_DV = 128
C_CONV = 4
C_CHUNK = 64
D_FF = 4096

LANES = 128
UA_W = 1536
UB_W = 768
UC_W = 3072
UG_W = 3072
SM_W = 128
W_ALL = UA_W + UB_W + UC_W + UG_W + SM_W
VMEM_LIMIT = 56 * 1024 * 1024


def _cparams(sem):
    return pltpu.CompilerParams(dimension_semantics=sem, vmem_limit_bytes=VMEM_LIMIT)


def _rms(x, gain):
    return x * lax.rsqrt(jnp.mean(x * x, axis=-1, keepdims=True) + EPS) * gain


def _inproj_kernel(x_ref, g_ref, wm_ref, wg_ref, ws_ref, b_ref, cos_ref, s1_ref, s2_ref, cw_ref,
                   ua_ref, ua4_ref, ua16_ref, ub_ref, uc_ref, ug_ref, sm_ref, stage_ref, ext_ref, h_ref,
                   *, tiles_per_seq):
    h_ref[...] = _rms(x_ref[...], g_ref[...]).astype(BF16)
    cosv, s1, s2 = cos_ref[...], s1_ref[...], s2_ref[...]
    w_main = UA_W + UB_W + UC_W

    def seg(off, width):
        if off < w_main:
            w = wm_ref[:, off:off + width]
        elif off < w_main + UG_W:
            w = wg_ref[:, off - w_main:off - w_main + width]
        else:
            w = ws_ref[...]
        return jnp.dot(h_ref[...], w, preferred_element_type=F32) + b_ref[:, off:off + width]

    def rope(a, scale):
        parts = []
        for g in range(a.shape[1] // LANES):
            t = a[:, g * LANES:(g + 1) * LANES]
            r = t * cosv + pltpu.roll(t, LANES - 8, 1) * s1 + pltpu.roll(t, 8, 1) * s2
            parts.append(r * scale if scale != 1.0 else r)
        return jnp.concatenate(parts, axis=1)

    qscale = HEAD_DIM ** -0.5
    tm = x_ref.shape[0]

    def seg_a(c):
        val = seg(c * 512, 512)
        if c < 2:
            val = rope(val, qscale if c == 0 else 1.0)
        ua_ref[:, c * 512:(c + 1) * 512] = val.astype(BF16)
        for gl in range(4):
            g = 4 * c + gl
            stage_ref[g] = val[:, gl * LANES:(gl + 1) * LANES]
            for dil, dst in ((4, ua4_ref), (16, ua16_ref)):
                for r in range(dil):
                    col = r * UA_W + g * LANES
                    dst[:, col:col + LANES] = stage_ref[g, pl.ds(r, tm // dil, stride=dil), :].astype(BF16)

    def seg_b():
        ub_ref[:, 0:512] = rope(seg(UA_W, 512), qscale).astype(BF16)
        ub_ref[:, 512:640] = rope(seg(UA_W + 512, 128), 1.0).astype(BF16)
        ub_ref[:, 640:768] = seg(UA_W + 640, 128).astype(BF16)

    def silu(t):
        return t * (0.5 + 0.5 * jnp.tanh(0.5 * t))

    @pl.when(pl.program_id(0) % tiles_per_seq == 0)
    def _():
        ext_ref[:, 0:8, :] = jnp.zeros((ext_ref.shape[0], 8, LANES), F32)

    def seg_conv(c):
        val = seg(UA_W + UB_W + c * 512, 512)
        for gl in range(4):
            g = 4 * c + gl
            sl = slice(g * LANES, (g + 1) * LANES)
            cur = val[:, gl * LANES:(gl + 1) * LANES]
            ext_ref[g, 8:8 + tm, :] = cur
            y = cur * cw_ref[C_CONV - 1:C_CONV, sl]
            for j in range(C_CONV - 1):
                y = y + ext_ref[g, 5 + j:5 + j + tm, :] * cw_ref[j:j + 1, sl]
            ext_ref[g, 0:8, :] = cur[tm - 8:tm]
            y = silu(y)
            if c < 2:
                scale = C_DK ** -0.5 if c == 0 else 1.0
                y = y * (lax.rsqrt(jnp.sum(y * y, axis=-1, keepdims=True) + EPS) * scale)
            uc_ref[:, sl] = y.astype(BF16)

    def seg_z(c):
        uc_ref[:, c * 512:(c + 1) * 512] = silu(seg(UA_W + UB_W + c * 512, 512)).astype(BF16)

    def seg_gate(c):
        ug_ref[:, c * 256:(c + 1) * 256] = jax.nn.sigmoid(seg(UA_W + UB_W + UC_W + c * 256, 256)).astype(BF16)

    def seg_small():
        sm_ref[...] = seg(UA_W + UB_W + UC_W + UG_W, SM_W)

    heavy = ([functools.partial(seg_a, c) for c in range(3)] + [seg_b]
             + [functools.partial(seg_conv, c) for c in range(4)]
             + [functools.partial(seg_z, c) for c in (4, 5)] + [seg_small])
    gates = [functools.partial(seg_gate, c) for c in range(12)]
    for k, fn in enumerate(heavy):
        fn()
        for gfn in gates[k * len(gates) // len(heavy):(k + 1) * len(gates) // len(heavy)]:
            gfn()


def _inproj(x2, gain, w_main, w_gate, w_small, b_all, cosv, s1, s2, conv_w, seq, tm):
    m = x2.shape[0]
    row = lambda i: (i, 0)
    const = lambda i: (0, 0)
    return pl.pallas_call(
        functools.partial(_inproj_kernel, tiles_per_seq=seq // tm),
        grid=(m // tm,),
        in_specs=[
            pl.BlockSpec((tm, D_MODEL), row),
            pl.BlockSpec((1, D_MODEL), const),
            pl.BlockSpec((D_MODEL, UA_W + UB_W + UC_W), const, pipeline_mode=pl.Buffered(1)),
            pl.BlockSpec((D_MODEL, UG_W), const, pipeline_mode=pl.Buffered(1)),
            pl.BlockSpec((D_MODEL, SM_W), const, pipeline_mode=pl.Buffered(1)),
            pl.BlockSpec((1, W_ALL), const),
            pl.BlockSpec((tm, LANES), row),
            pl.BlockSpec((tm, LANES), row),
            pl.BlockSpec((tm, LANES), row),
            pl.BlockSpec((C_CONV, 2048), const),
        ],
        out_specs=[
            pl.BlockSpec((tm, UA_W), row),
            pl.BlockSpec((tm // 4, 4 * UA_W), row),
            pl.BlockSpec((tm // 16, 16 * UA_W), row),
            pl.BlockSpec((tm, UB_W), row),
            pl.BlockSpec((tm, UC_W), row),
            pl.BlockSpec((tm, UG_W), row),
            pl.BlockSpec((tm, SM_W), row),
        ],
        out_shape=[
            jax.ShapeDtypeStruct((m, UA_W), BF16),
            jax.ShapeDtypeStruct((m // 4, 4 * UA_W), BF16),
            jax.ShapeDtypeStruct((m // 16, 16 * UA_W), BF16),
            jax.ShapeDtypeStruct((m, UB_W), BF16),
            jax.ShapeDtypeStruct((m, UC_W), BF16),
            jax.ShapeDtypeStruct((m, UG_W), BF16),
            jax.ShapeDtypeStruct((m, SM_W), F32),
        ],
        scratch_shapes=[pltpu.VMEM((UA_W // LANES, tm, LANES), F32),
                        pltpu.VMEM((2048 // LANES, tm + 8, LANES), F32),
                        pltpu.VMEM((tm, D_MODEL), BF16)],
        compiler_params=_cparams(("arbitrary",)),
        name="inproj",
    )(x2, gain, w_main, w_gate, w_small, b_all, cosv, s1, s2, conv_w)


ATT_QT = 512


def _band_valid(max_dist, first_block):
    qi = lax.broadcasted_iota(jnp.int32, (BLK, 2 * BLK), 0)
    kj = lax.broadcasted_iota(jnp.int32, (BLK, 2 * BLK), 1)
    dist = BLK + qi - kj
    band = (dist >= 0) & (dist <= max_dist)
    if first_block is None:
        return band
    return band & (jnp.logical_not(first_block) | (kj >= BLK))


def _kv_window(cur_ref, prev_ref, i, sl):
    if i == 0:
        return jnp.concatenate([prev_ref[0, :, sl], cur_ref[0, 0:BLK, sl]], axis=0)
    return cur_ref[0, (i - 1) * BLK:(i + 1) * BLK, sl]


ATT_LOOKAHEAD = 2


def _attention_tile(nblk, get_q, get_k, get_v, get_valid, get_sink, emit):
    lane = lax.broadcasted_iota(jnp.int32, (BLK, LANES), 1)
    first = lane < HEAD_DIM
    units = [(i, p, hh) for i in range(nblk) for p in range(4) for hh in range(2)]

    def scores(u):
        i, p, hh = u
        qp = get_q(i, p)
        qm = jnp.where(first if hh == 0 else ~first, qp, jnp.zeros_like(qp))
        return lax.dot_general(qm, get_k(i, p), (((1,), (1,)), ((), ())), preferred_element_type=F32)

    def finish(u, s):
        i, p, hh = u
        sink = get_sink(p, hh)
        s = jnp.where(get_valid(i), s, NEG_INF)
        m = jnp.max(s, axis=-1, keepdims=True)
        if sink is not None:
            m = jnp.maximum(m, sink)
        pr = jnp.exp(s - m)
        den = jnp.sum(pr, axis=-1, keepdims=True)
        if sink is not None:
            den = den + jnp.exp(sink - m)
        return jnp.dot(pr.astype(BF16), get_v(i, p), preferred_element_type=F32), m, den

    pending = {k: scores(units[k]) for k in range(min(ATT_LOOKAHEAD, len(units)))}
    done = {}
    for k, u in enumerate(units):
        if k + ATT_LOOKAHEAD < len(units):
            pending[k + ATT_LOOKAHEAD] = scores(units[k + ATT_LOOKAHEAD])
        done[u] = finish(u, pending.pop(k))
        i, p, hh = u
        if hh == 1:
            (o0, m0, d0), (o1, m1, d1) = done.pop((i, p, 0)), done.pop((i, p, 1))
            den = jnp.where(first, d0, d1)
            emit(i, p, jnp.where(first, o0, o1) / den, jnp.where(first, m0, m1) + jnp.log(den))


def _attn_a_kernel(q_ref, kc_ref, kp_ref, vc_ref, vp_ref, o_ref, lse_ref, *, max_dist):
    first_tile = pl.program_id(2) == 0
    lanes = lambda p: slice(p * LANES, (p + 1) * LANES)
    rows = lambda i: slice(i * BLK, (i + 1) * BLK)
    valid = {}

    def get_valid(i):
        key = min(i, 1)
        if key not in valid:
            valid[key] = _band_valid(max_dist, first_tile if i == 0 else None)
        return valid[key]

    def emit(i, p, o, lse):
        o_ref[0, rows(i), lanes(p)] = o.astype(BF16)
        lse_ref[0, rows(i), lanes(p)] = lse

    _attention_tile(q_ref.shape[1] // BLK,
                    lambda i, p: q_ref[0, rows(i), lanes(p)],
                    lambda i, p: _kv_window(kc_ref, kp_ref, i, lanes(p)),
                    lambda i, p: _kv_window(vc_ref, vp_ref, i, lanes(p)),
                    get_valid, lambda p, hh: None, emit)


def _attn_a(ua_d, bsz, seq, window, dil):
    l = seq // dil
    steps = window // dil
    view = ua_d.reshape(bsz, l, dil * UA_W)
    qt = min(ATT_QT, l)
    blk = (1, qt, 512)
    pblk = (1, BLK, 512)
    cur = lambda c: (lambda b, r, j: (b, j, 3 * r + c))
    prev = lambda c: (lambda b, r, j: (b, jnp.maximum(j * (qt // BLK) - 1, 0), 3 * r + c))
    out = lambda b, r, j: (b, j, r)
    o, lse = pl.pallas_call(
        functools.partial(_attn_a_kernel, max_dist=steps),
        grid=(bsz, dil, l // qt),
        in_specs=[pl.BlockSpec(blk, cur(0)), pl.BlockSpec(blk, cur(1)), pl.BlockSpec(pblk, prev(1)),
                  pl.BlockSpec(blk, cur(2)), pl.BlockSpec(pblk, prev(2))],
        out_specs=[pl.BlockSpec(blk, out), pl.BlockSpec(blk, out)],
        out_shape=[jax.ShapeDtypeStruct((bsz, l, dil * 512), BF16),
                   jax.ShapeDtypeStruct((bsz, l, dil * 512), F32)],
        compiler_params=_cparams(("parallel", "parallel", "arbitrary")),
        name=f"attn_a_d{dil}",
    )(view, view, view, view, view)
    return o.reshape(bsz * l, dil * 512), lse.reshape(bsz * l, dil * 512)


def _attn_b_kernel(sink_ref, q_ref, kc_ref, kp_ref, vc_ref, vp_ref, o_ref):
    first_tile = pl.program_id(1) == 0
    lane = lax.broadcasted_iota(jnp.int32, (2 * BLK, LANES), 1)
    first = lane < HEAD_DIM
    lanes = lambda p: slice(p * LANES, (p + 1) * LANES)
    rows = lambda i: slice(i * BLK, (i + 1) * BLK)
    valid, dup = {}, {}

    def get_valid(i):
        key = min(i, 1)
        if key not in valid:
            valid[key] = _band_valid(B_WINDOW - 1, first_tile if i == 0 else None)
        return valid[key]

    def get_dup(cur_ref, prev_ref, i, kv):
        key = (id(cur_ref), i)
        if key not in dup:
            t = _kv_window(cur_ref, prev_ref, i, slice(0, LANES))
            t_sw = pltpu.roll(t, HEAD_DIM, 1)
            dup[key] = (jnp.where(first, t, t_sw), jnp.where(first, t_sw, t))
        return dup[key][kv]

    def emit(i, p, o, lse):
        o_ref[0, rows(i), lanes(p)] = o.astype(BF16)

    _attention_tile(q_ref.shape[1] // BLK,
                    lambda i, p: q_ref[0, rows(i), lanes(p)],
                    lambda i, p: get_dup(kc_ref, kp_ref, i, p // 2),
                    lambda i, p: get_dup(vc_ref, vp_ref, i, p // 2),
                    get_valid, lambda p, hh: sink_ref[2 * p + hh], emit)


def _attn_b(ub, sinks, bsz, seq):
    view = ub.reshape(bsz, seq, UB_W)
    qt = ATT_QT
    cur = lambda c: (lambda b, j: (b, j, c))
    prev = lambda c: (lambda b, j: (b, jnp.maximum(j * (qt // BLK) - 1, 0), c))
    o = pl.pallas_call(
        _attn_b_kernel,
        grid=(bsz, seq // qt),
        in_specs=[pl.BlockSpec(memory_space=pltpu.SMEM),
                  pl.BlockSpec((1, qt, 512), lambda b, j: (b, j, 0)),
                  pl.BlockSpec((1, qt, LANES), cur(4)), pl.BlockSpec((1, BLK, LANES), prev(4)),
                  pl.BlockSpec((1, qt, LANES), cur(5)), pl.BlockSpec((1, BLK, LANES), prev(5))],
        out_specs=pl.BlockSpec((1, qt, 512), lambda b, j: (b, j, 0)),
        out_shape=jax.ShapeDtypeStruct((bsz, seq, 512), BF16),
        compiler_params=_cparams(("parallel", "arbitrary")),
        name="attn_b",
    )(sinks, view, view, view, view, view)
    return o.reshape(bsz * seq, 512)


def _split3(a):
    hi = a.astype(BF16)
    r = a - hi.astype(F32)
    mid = r.astype(BF16)
    lo = (r - mid.astype(F32)).astype(BF16)
    return hi, mid, lo


def _dot_nt(a, b):
    return lax.dot_general(a, b, (((1,), (1,)), ((), ())), preferred_element_type=F32)


def _dot_tn(a, b):
    return lax.dot_general(a, b, (((0,), (0,)), ((), ())), preferred_element_type=F32)


def _mm(a, b):
    return jnp.dot(a.astype(BF16), b.astype(BF16), preferred_element_type=F32)


GDN_TT = 256
GDN_NCH = GDN_TT // C_CHUNK
GDN_CPI = 2
GDN_NBAT = 4


def _gdn_kernel(cq_ref, ck_ref, cv_ref, zs_ref, sm_ref, alog_ref, dtb_ref, cn_ref, e_ref, lbd_ref,
                y_ref, state_ref, expd_ref, u_ref, w_ref, qd_ref, kd_ref, attn_ref):
    c, tt = C_CHUNK, GDN_TT

    @pl.when(pl.program_id(1) == 0)
    def _():
        state_ref[...] = jnp.zeros_like(state_ref)

    nbat = sm_ref.shape[0]
    lane = lax.broadcasted_iota(jnp.int32, (tt, LANES), 1)
    lbd = lbd_ref[...]
    for nb in range(nbat):
        sm = sm_ref[nb]
        z = sm + dtb_ref[...]
        softplus = jnp.maximum(z, 0.0) + jnp.log1p(jnp.exp(-jnp.abs(z)))
        g = -jnp.exp(alog_ref[...]) * softplus
        beta = jax.nn.sigmoid(sm)
        g_hi, g_mid, g_lo = _split3(g)
        gc = (jnp.dot(lbd, g_hi, preferred_element_type=F32) + jnp.dot(lbd, g_mid, preferred_element_type=F32)
              + jnp.dot(lbd, g_lo, preferred_element_type=F32))
        c_hi, c_mid, c_lo = _split3(jnp.where(lane < C_V_HEADS, gc, beta))
        pieces = jnp.where(lane < 16, c_hi,
                           jnp.where(lane < 32, pltpu.roll(c_mid, 16, 1),
                                     jnp.where(lane < 48, pltpu.roll(c_lo, 32, 1), jnp.zeros_like(c_hi))))
        for s in range(4):
            sl = slice(s * 512, (s + 1) * 512)
            expd_ref[nb, :, sl] = jnp.dot(pieces, e_ref[:, sl], preferred_element_type=F32)

    ri = lax.broadcasted_iota(jnp.int32, (c, LANES), 0)
    lj = lax.broadcasted_iota(jnp.int32, (c, LANES), 1)
    cj = lj & (c - 1)
    left = lj < c
    tril = ri >= cj
    strict = ri > cj
    eye = (ri == cj).astype(F32)
    bi = lax.broadcasted_iota(jnp.int32, (2 * c, LANES), 0)
    bj = lax.broadcasted_iota(jnp.int32, (2 * c, LANES), 1)
    blockdiag = (bi < c) == (bj < c)

    def bd(m2):
        m16 = m2.astype(BF16)
        return jnp.where(blockdiag, jnp.concatenate([m16, m16], axis=0), jnp.zeros((2 * c, LANES), BF16))

    def chunk_local(i, carry):
        units = []
        for nb, j in [(nb, j) for nb in range(nbat) for j in range(cpi)]:
            r0 = pl.multiple_of((i * cpi + j) * c, c)
            rows = pl.ds(r0, c)
            for p in range(C_V_HEADS // 2):
                hs = [slice(h * LANES, (h + 1) * LANES) for h in (2 * p, 2 * p + 1)]
                ps = slice(p * C_DK, (p + 1) * C_DK)
                units.append(dict(nb=nb, rows=rows, p=p, hs=hs, q16=cq_ref[nb, rows, ps], k16=ck_ref[nb, rows, ps],
                                  v=[cv_ref[nb, rows, s].astype(F32) for s in hs],
                                  ge=[expd_ref[nb, rows, s] for s in hs],
                                  be=[expd_ref[nb, rows, 1024 + s.start:1024 + s.stop] for s in hs]))
        for un in units:
            q16, k16 = un["q16"], un["k16"]
            un["qn"], un["kn"] = q16.astype(F32), k16.astype(F32)
            un["kq"] = _dot_nt(jnp.concatenate([k16, q16], axis=0),
                               jnp.concatenate([k16, k16], axis=0))
        for un in units:
            ge, be, kq = un["ge"], un["be"], un["kq"]
            grow = jnp.concatenate(ge, axis=0).T[0:c, :]
            diff = jnp.where(left, ge[0], ge[1]) - grow
            dec = jnp.where(tril, jnp.exp(jnp.where(tril, diff, 0.0)), 0.0)
            un["pw"] = jnp.where(strict, kq[0:c] * dec * jnp.where(left, be[0], be[1]), 0.0) * -1.0
            un["attn"] = jnp.where(tril, kq[c:] * dec, 0.0).astype(BF16)
            un["x"] = eye + un["pw"]
        for un in units:
            un["pw"] = _mm(un["pw"], bd(un["pw"]))
        for _ in range(4):
            for un in units:
                r = _mm(jnp.concatenate([un["pw"], un["x"]], axis=0), bd(un["pw"]))
                un["x"] = un["x"] + r[c:]
                un["pw"] = r[0:c]
        for un in units:
            un["x"] = un["x"] + _mm(un["x"], bd(un["pw"]))
        zero = jnp.zeros((c, 2 * C_DV), F32)
        for un in units:
            rhs, un["qd"], un["kd"] = [], [], []
            for ge, be, v in zip(un["ge"], un["be"], un["v"]):
                eg = jnp.exp(ge)
                rhs.append(jnp.concatenate([v * be, un["kn"] * be * eg], axis=1))
                un["qd"].append((un["qn"] * eg).astype(BF16))
                un["kd"].append((un["kn"] * jnp.exp(ge[c - 1:c, :] - ge)).astype(BF16))
            bdr = jnp.concatenate([jnp.concatenate([rhs[0], zero], axis=1),
                                   jnp.concatenate([zero, rhs[1]], axis=1)], axis=0)
            un["uw"] = _mm(un["x"], bdr)
        for un in units:
            nb, rows, p = un["nb"], un["rows"], un["p"]
            for t, s in enumerate(un["hs"]):
                u_ref[nb, rows, s] = un["uw"][:, 2 * t * C_DV:(2 * t + 1) * C_DV]
                w_ref[nb, rows, s] = un["uw"][:, (2 * t + 1) * C_DV:(2 * t + 2) * C_DV].astype(BF16)
                qd_ref[nb, rows, s] = un["qd"][t]
                kd_ref[nb, rows, s] = un["kd"][t]
            attn_ref[nb, rows, p * LANES:(p + 1) * LANES] = un["attn"]
        return carry

    cpi = max(1, GDN_CPI // nbat)
    lax.fori_loop(0, GDN_NCH // cpi, chunk_local, 0)
    cn = cn_ref[...]

    def recur(i, carry):
        r0 = pl.multiple_of(i * c, c)
        rows = pl.ds(r0, c)
        last8 = pl.ds(pl.multiple_of(r0 + c - 8, 8), 8)
        hsl = [slice(h * C_DV, (h + 1) * C_DV) for h in range(C_V_HEADS)]
        chains = [(nb, h) for nb in range(nbat) for h in range(C_V_HEADS)]
        s = {k: state_ref[k[0] * C_V_HEADS + k[1]] for k in chains}
        wq = {(nb, h): jnp.dot(jnp.concatenate([w_ref[nb, rows, hsl[h]], qd_ref[nb, rows, hsl[h]]], axis=0),
                               s[nb, h].astype(BF16), preferred_element_type=F32) for nb, h in chains}
        v16 = {(nb, h): (u_ref[nb, rows, hsl[h]] - wq[nb, h][0:c]).astype(BF16) for nb, h in chains}
        s_new = {(nb, h): s[nb, h] * jnp.exp(expd_ref[nb, last8, hsl[h]][7:8]) + _dot_tn(kd_ref[nb, rows, hsl[h]], v16[nb, h])
                 for nb, h in chains}
        zero = jnp.zeros((c, C_DV), BF16)
        o = {}
        for nb in range(nbat):
            for p in range(C_V_HEADS // 2):
                bdv = jnp.concatenate([jnp.concatenate([v16[nb, 2 * p], zero], axis=1),
                                       jnp.concatenate([zero, v16[nb, 2 * p + 1]], axis=1)], axis=0)
                av = jnp.dot(attn_ref[nb, rows, p * LANES:(p + 1) * LANES], bdv, preferred_element_type=F32)
                o[nb, 2 * p] = wq[nb, 2 * p][c:] + av[:, 0:C_DV]
                o[nb, 2 * p + 1] = wq[nb, 2 * p + 1][c:] + av[:, C_DV:]
        for nb, h in chains:
            state_ref[nb * C_V_HEADS + h] = s_new[nb, h]
            y_ref[nb, rows, hsl[h]] = (_rms(o[nb, h], cn) * zs_ref[nb, rows, hsl[h]].astype(F32)).astype(BF16)
        return carry

    lax.fori_loop(0, GDN_NCH, recur, 0)


def _gdn(uc, sm, a_log, dt_bias, c_norm, bsz, seq):
    c, tt = C_CHUNK, GDN_TT
    nt = seq // tt
    pad = lambda v: jnp.pad(v.astype(F32), (0, LANES - v.shape[0])).reshape(1, LANES)
    li = np.arange(LANES)[:, None]
    cj = np.arange(2048)[None, :]
    lp = li % 16
    e = jnp.asarray((li < 48) & (((lp < 8) & (cj < 1024) & (cj // LANES == lp))
                                 | ((lp >= 8) & (cj >= 1024) & ((cj - 1024) // LANES == lp - 8))), BF16)
    ti = np.arange(tt)
    lbd = jnp.asarray((ti[:, None] // c == ti[None, :] // c) & (ti[:, None] >= ti[None, :]), BF16)
    nbat = GDN_NBAT if bsz % GDN_NBAT == 0 else 1
    uc3 = uc.reshape(bsz, seq, UC_W)
    sm3 = sm.reshape(bsz, seq, SM_W)
    row = lambda blockcol: (lambda b, t: (b, t, blockcol))
    const = lambda b, t: (0, 0)
    yc = pl.pallas_call(
        _gdn_kernel,
        grid=(bsz // nbat, nt),
        in_specs=[
            pl.BlockSpec((nbat, tt, 512), row(0)),
            pl.BlockSpec((nbat, tt, 512), row(1)),
            pl.BlockSpec((nbat, tt, 1024), row(1)),
            pl.BlockSpec((nbat, tt, 1024), row(2)),
            pl.BlockSpec((nbat, tt, SM_W), row(0)),
            pl.BlockSpec((1, LANES), const),
            pl.BlockSpec((1, LANES), const),
            pl.BlockSpec((1, LANES), const),
            pl.BlockSpec((LANES, 2048), const),
            pl.BlockSpec((tt, tt), const),
        ],
        out_specs=pl.BlockSpec((nbat, tt, 1024), row(0)),
        out_shape=jax.ShapeDtypeStruct((bsz, seq, 1024), BF16),
        scratch_shapes=[
            pltpu.VMEM((nbat * C_V_HEADS, C_DK, C_DV), F32),
            pltpu.VMEM((nbat, tt, 2048), F32),
            pltpu.VMEM((nbat, tt, 1024), F32),
            pltpu.VMEM((nbat, tt, 1024), BF16),
            pltpu.VMEM((nbat, tt, 1024), BF16),
            pltpu.VMEM((nbat, tt, 1024), BF16),
            pltpu.VMEM((nbat, tt, 512), BF16),
        ],
        compiler_params=_cparams(("parallel", "arbitrary")),
        name="gdn",
    )(uc3, uc3, uc3, uc3, sm3, pad(a_log), pad(dt_bias), c_norm.astype(F32).reshape(1, LANES), e, lbd)
    return yc.reshape(bsz * seq, 1024)


def _merge_kernel(x_ref, o1_ref, o2_ref, o3_ref, l1_ref, l2_ref, l3_ref, yb_ref, yc_ref, ug_ref,
                  wa_ref, wb_ref, wc_ref, wo_ref, out_ref, nat_ref):
    tm = x_ref.shape[0]
    for slot, (dil, src) in enumerate(((4, o2_ref), (4, l2_ref), (16, o3_ref), (16, l3_ref))):
        for r in range(dil):
            for g in range(4):
                col = r * 512 + g * LANES
                nat_ref[slot, g, pl.ds(r, tm // dil, stride=dil), :] = src[:, col:col + LANES].astype(F32)
    parts = []
    for g in range(4):
        sl = slice(g * LANES, (g + 1) * LANES)
        l1, l2, l3 = l1_ref[:, sl], nat_ref[1, g], nat_ref[3, g]
        m = jnp.maximum(jnp.maximum(l1, l2), l3)
        e1, e2, e3 = jnp.exp(l1 - m), jnp.exp(l2 - m), jnp.exp(l3 - m)
        parts.append(((e1 * o1_ref[:, sl].astype(F32) + e2 * nat_ref[0, g] + e3 * nat_ref[2, g])
                      / (e1 + e2 + e3)).astype(BF16))
    ya = jnp.concatenate(parts, axis=1)
    merged = (ug_ref[:, 0:1024].astype(F32) * jnp.dot(ya, wa_ref[...], preferred_element_type=F32)
              + ug_ref[:, 1024:2048].astype(F32) * jnp.dot(yb_ref[...], wb_ref[...], preferred_element_type=F32)
              + ug_ref[:, 2048:3072].astype(F32) * jnp.dot(yc_ref[...], wc_ref[...], preferred_element_type=F32))
    out_ref[...] = x_ref[...] + jnp.dot(merged.astype(BF16), wo_ref[...], preferred_element_type=F32)


def _merge(x2, o_list, lse_list, yb, yc, ug, wa, wb, wc, wo, tm):
    m = x2.shape[0]
    row = lambda i: (i, 0)
    const = lambda i: (0, 0)
    rs = lambda w, dil=1: pl.BlockSpec((tm // dil, dil * w), row)
    return pl.pallas_call(
        _merge_kernel,
        grid=(m // tm,),
        in_specs=[rs(D_MODEL), rs(512), rs(512, 4), rs(512, 16), rs(512), rs(512, 4), rs(512, 16),
                  rs(512), rs(1024), rs(UG_W),
                  pl.BlockSpec((512, D_MODEL), const), pl.BlockSpec((512, D_MODEL), const),
                  pl.BlockSpec((1024, D_MODEL), const), pl.BlockSpec((D_MODEL, D_MODEL), const)],
        out_specs=rs(D_MODEL),
        out_shape=jax.ShapeDtypeStruct((m, D_MODEL), F32),
        scratch_shapes=[pltpu.VMEM((4, 4, tm, LANES), F32)],
        compiler_params=_cparams(("parallel",)),
        name="merge",
    )(x2, *o_list, *lse_list, yb, yc, ug, wa, wb, wc, wo)


def _ffn_kernel(x_ref, g_ref, w1_ref, w2_ref, gf_ref, out_ref, *, final_norm):
    x = x_ref[...]
    h = _rms(x, g_ref[...]).astype(BF16)
    acc = x
    for c in range(D_FF // 1024):
        sl = slice(c * 1024, (c + 1) * 1024)
        a = jnp.maximum(jnp.dot(h, w1_ref[:, sl], preferred_element_type=F32), 0.0)
        acc = acc + jnp.dot((a * a).astype(BF16), w2_ref[sl, :], preferred_element_type=F32)
    out_ref[...] = _rms(acc, gf_ref[...]) if final_norm else acc


def _ffn(x2, gain, w1, w2, gain_final, final_norm, tm):
    m = x2.shape[0]
    row = lambda i: (i, 0)
    const = lambda i: (0, 0)
    return pl.pallas_call(
        functools.partial(_ffn_kernel, final_norm=final_norm),
        grid=(m // tm,),
        in_specs=[pl.BlockSpec((tm, D_MODEL), row), pl.BlockSpec((1, D_MODEL), const),
                  pl.BlockSpec((D_MODEL, D_FF), const), pl.BlockSpec((D_FF, D_MODEL), const),
                  pl.BlockSpec((1, D_MODEL), const)],
        out_specs=pl.BlockSpec((tm, D_MODEL), row),
        out_shape=jax.ShapeDtypeStruct((m, D_MODEL), F32),
        compiler_params=_cparams(("parallel",)),
        name="ffn",
    )(x2, gain, w1, w2, gain_final)


def _rope_tables(positions):
    half = ROT_DIM // 2
    inv_freq = jnp.power(ROPE_THETA, -jnp.arange(0, ROT_DIM, 2, dtype=F32) / ROT_DIM)
    rest = jnp.zeros((HEAD_DIM - ROT_DIM,), F32)
    zero8 = jnp.zeros((half,), F32)
    two = lambda t: jnp.concatenate([t, t])
    freq = two(jnp.concatenate([inv_freq, inv_freq, rest]))
    m1 = two(jnp.concatenate([-jnp.ones((half,), F32), zero8, rest]))
    m2 = two(jnp.concatenate([zero8, jnp.ones((half,), F32), rest]))
    ang = positions.astype(F32).reshape(-1, 1) * freq
    sin = jnp.sin(ang)
    return jnp.cos(ang), sin * m1, sin * m2


def kernel(x, positions, norm_mix, w_in, b_in, conv_w, a_log, dt_bias, sinks, c_norm, w_branch_a, w_branch_b,
           w_branch_c, w_out, norm_ffn, w_ff1, w_ff2, norm_final):
    bsz, seq, d = x.shape
    depth = w_in.shape[0]
    assert d == D_MODEL and seq % (16 * BLK) == 0
    m = bsz * seq
    cosv, s1, s2 = _rope_tables(positions)
    x2 = x.reshape(m, d)
    ca = 5376
    for layer in range(depth):
        wl, bl = w_in[layer], b_in[layer]
        w_main = wl[:, :ca].astype(BF16)
        w_gate = wl[:, ca + 16:].astype(BF16)
        w_small = jnp.pad(wl[:, ca:ca + 16], ((0, 0), (0, SM_W - 16))).astype(BF16)
        b_all = jnp.concatenate([bl[:ca], bl[ca + 16:], bl[ca:ca + 16], jnp.zeros((SM_W - 16,), F32)]).reshape(1, -1)
        ua, ua4, ua16, ub, uc, ug, sm = _inproj(x2, norm_mix[layer].reshape(1, d), w_main, w_gate, w_small, b_all,
                                                cosv, s1, s2, conv_w[layer].astype(F32), seq, tm=512)
        o_list, lse_list = [], []
        for (window, dil), ua_d in zip(A_CONFIGS, (ua, ua4, ua16)):
            o, lse = _attn_a(ua_d, bsz, seq, window, dil)
            o_list.append(o)
            lse_list.append(lse)
        yb = _attn_b(ub, sinks[layer].astype(F32), bsz, seq)
        yc = _gdn(uc, sm, a_log[layer], dt_bias[layer], c_norm[layer], bsz, seq)
        x2 = _merge(x2, o_list, lse_list, yb, yc, ug, w_branch_a[layer].astype(BF16), w_branch_b[layer].astype(BF16),
                    w_branch_c[layer].astype(BF16), w_out[layer].astype(BF16), tm=512)
        x2 = _ffn(x2, norm_ffn[layer].reshape(1, d), w_ff1[layer].astype(BF16), w_ff2[layer].astype(BF16),
                  norm_final.reshape(1, d), final_norm=(layer == depth - 1), tm=512)
    return x2.reshape(bsz, seq, d)
```

```python
import functools

import jax
import jax.numpy as jnp
import numpy as np
from jax import lax
from jax.experimental import pallas as pl
from jax.experimental.pallas import tpu as pltpu

F32 = jnp.float32
BF16 = jnp.bfloat16

D_MODEL = 1024
HEAD_DIM = 64
ROT_DIM = 16
ROPE_THETA = 500000.0
BLK = 128
NEG_INF = -1e30
EPS = 1e-6
A_CONFIGS = ((128, 1), (512, 4), (2048, 16))
B_WINDOW = 128
C_V_HEADS = 8
C_DK = 128
C_DV = 128
C_CONV = 4
C_CHUNK = 64
D_FF = 4096

LANES = 128
UA_W = 1536
UB_W = 768
UC_W = 3072
UG_W = 3072
SM_W = 128
W_ALL = UA_W + UB_W + UC_W + UG_W + SM_W
VMEM_LIMIT = 56 * 1024 * 1024


def _cparams(sem):
    return pltpu.CompilerParams(dimension_semantics=sem, vmem_limit_bytes=VMEM_LIMIT)


def _rms(x, gain):
    return x * lax.rsqrt(jnp.mean(x * x, axis=-1, keepdims=True) + EPS) * gain


def _inproj_kernel(x_ref, g_ref, wm_ref, wg_ref, ws_ref, b_ref, cos_ref, s1_ref, s2_ref, cw_ref,
                   ua_ref, ua4_ref, ua16_ref, ub_ref, uc_ref, ug_ref, sm_ref, stage_ref, ext_ref, h_ref,
                   *, tiles_per_seq):
    h_ref[...] = _rms(x_ref[...], g_ref[...]).astype(BF16)
    cosv, s1, s2 = cos_ref[...], s1_ref[...], s2_ref[...]
    w_main = UA_W + UB_W + UC_W

    def seg(off, width):
        if off < w_main:
            w = wm_ref[:, off:off + width]
        elif off < w_main + UG_W:
            w = wg_ref[:, off - w_main:off - w_main + width]
        else:
            w = ws_ref[...]
        return jnp.dot(h_ref[...], w, preferred_element_type=F32) + b_ref[:, off:off + width]

    def rope(a, scale):
        parts = []
        for g in range(a.shape[1] // LANES):
            t = a[:, g * LANES:(g + 1) * LANES]
            r = t * cosv + pltpu.roll(t, LANES - 8, 1) * s1 + pltpu.roll(t, 8, 1) * s2
            parts.append(r * scale if scale != 1.0 else r)
        return jnp.concatenate(parts, axis=1)

    qscale = HEAD_DIM ** -0.5
    tm = x_ref.shape[0]

    def seg_a(c):
        val = seg(c * 512, 512)
        if c < 2:
            val = rope(val, qscale if c == 0 else 1.0)
        ua_ref[:, c * 512:(c + 1) * 512] = val.astype(BF16)
        for gl in range(4):
            g = 4 * c + gl
            stage_ref[g] = val[:, gl * LANES:(gl + 1) * LANES]
            for dil, dst in ((4, ua4_ref), (16, ua16_ref)):
                for r in range(dil):
                    col = r * UA_W + g * LANES
                    dst[:, col:col + LANES] = stage_ref[g, pl.ds(r, tm // dil, stride=dil), :].astype(BF16)

    def seg_b():
        ub_ref[:, 0:512] = rope(seg(UA_W, 512), qscale).astype(BF16)
        ub_ref[:, 512:640] = rope(seg(UA_W + 512, 128), 1.0).astype(BF16)
        ub_ref[:, 640:768] = seg(UA_W + 640, 128).astype(BF16)

    def silu(t):
        return t * (0.5 + 0.5 * jnp.tanh(0.5 * t))

    @pl.when(pl.program_id(0) % tiles_per_seq == 0)
    def _():
        ext_ref[:, 0:8, :] = jnp.zeros((ext_ref.shape[0], 8, LANES), F32)

    def seg_conv(c):
        val = seg(UA_W + UB_W + c * 512, 512)
        for gl in range(4):
            g = 4 * c + gl
            sl = slice(g * LANES, (g + 1) * LANES)
            cur = val[:, gl * LANES:(gl + 1) * LANES]
            ext_ref[g, 8:8 + tm, :] = cur
            y = cur * cw_ref[C_CONV - 1:C_CONV, sl]
            for j in range(C_CONV - 1):
                y = y + ext_ref[g, 5 + j:5 + j + tm, :] * cw_ref[j:j + 1, sl]
            ext_ref[g, 0:8, :] = cur[tm - 8:tm]
            y = silu(y)
            if c < 2:
                scale = C_DK ** -0.5 if c == 0 else 1.0
                y = y * (lax.rsqrt(jnp.sum(y * y, axis=-1, keepdims=True) + EPS) * scale)
            uc_ref[:, sl] = y.astype(BF16)

    def seg_z(c):
        uc_ref[:, c * 512:(c + 1) * 512] = silu(seg(UA_W + UB_W + c * 512, 512)).astype(BF16)

    def seg_gate(c):
        ug_ref[:, c * 256:(c + 1) * 256] = jax.nn.sigmoid(seg(UA_W + UB_W + UC_W + c * 256, 256)).astype(BF16)

    def seg_small():
        sm_ref[...] = seg(UA_W + UB_W + UC_W + UG_W, SM_W)

    heavy = ([functools.partial(seg_a, c) for c in range(3)] + [seg_b]
             + [functools.partial(seg_conv, c) for c in range(4)]
             + [functools.partial(seg_z, c) for c in (4, 5)] + [seg_small])
    gates = [functools.partial(seg_gate, c) for c in range(12)]
    for k, fn in enumerate(heavy):
        fn()
        for gfn in gates[k * len(gates) // len(heavy):(k + 1) * len(gates) // len(heavy)]:
            gfn()


def _inproj(x2, gain, w_main, w_gate, w_small, b_all, cosv, s1, s2, conv_w, seq, tm):
    m = x2.shape[0]
    row = lambda i: (i, 0)
    const = lambda i: (0, 0)
    return pl.pallas_call(
        functools.partial(_inproj_kernel, tiles_per_seq=seq // tm),
        grid=(m // tm,),
        in_specs=[
            pl.BlockSpec((tm, D_MODEL), row),
            pl.BlockSpec((1, D_MODEL), const),
            pl.BlockSpec((D_MODEL, UA_W + UB_W + UC_W), const, pipeline_mode=pl.Buffered(1)),
            pl.BlockSpec((D_MODEL, UG_W), const, pipeline_mode=pl.Buffered(1)),
            pl.BlockSpec((D_MODEL, SM_W), const, pipeline_mode=pl.Buffered(1)),
            pl.BlockSpec((1, W_ALL), const),
            pl.BlockSpec((tm, LANES), row),
            pl.BlockSpec((tm, LANES), row),
            pl.BlockSpec((tm, LANES), row),
            pl.BlockSpec((C_CONV, 2048), const),
        ],
        out_specs=[
            pl.BlockSpec((tm, UA_W), row),
            pl.BlockSpec((tm // 4, 4 * UA_W), row),
            pl.BlockSpec((tm // 16, 16 * UA_W), row),
            pl.BlockSpec((tm, UB_W), row),
            pl.BlockSpec((tm, UC_W), row),
            pl.BlockSpec((tm, UG_W), row),
            pl.BlockSpec((tm, SM_W), row),
        ],
        out_shape=[
            jax.ShapeDtypeStruct((m, UA_W), BF16),
            jax.ShapeDtypeStruct((m // 4, 4 * UA_W), BF16),
            jax.ShapeDtypeStruct((m // 16, 16 * UA_W), BF16),
            jax.ShapeDtypeStruct((m, UB_W), BF16),
            jax.ShapeDtypeStruct((m, UC_W), BF16),
            jax.ShapeDtypeStruct((m, UG_W), BF16),
            jax.ShapeDtypeStruct((m, SM_W), F32),
        ],
        scratch_shapes=[pltpu.VMEM((UA_W // LANES, tm, LANES), F32),
                        pltpu.VMEM((2048 // LANES, tm + 8, LANES), F32),
                        pltpu.VMEM((tm, D_MODEL), BF16)],
        compiler_params=_cparams(("arbitrary",)),
        name="inproj",
    )(x2, gain, w_main, w_gate, w_small, b_all, cosv, s1, s2, conv_w)


ATT_QT = 512


def _band_valid(max_dist, first_block):
    qi = lax.broadcasted_iota(jnp.int32, (BLK, 2 * BLK), 0)
    kj = lax.broadcasted_iota(jnp.int32, (BLK, 2 * BLK), 1)
    dist = BLK + qi - kj
    band = (dist >= 0) & (dist <= max_dist)
    if first_block is None:
        return band
    return band & (jnp.logical_not(first_block) | (kj >= BLK))


def _kv_window(cur_ref, prev_ref, i, sl):
    if i == 0:
        return jnp.concatenate([prev_ref[0, :, sl], cur_ref[0, 0:BLK, sl]], axis=0)
    return cur_ref[0, (i - 1) * BLK:(i + 1) * BLK, sl]


ATT_LOOKAHEAD = 4


def _attention_tile(nblk, get_q, get_k, get_v, get_valid, get_sink, emit):
    lane = lax.broadcasted_iota(jnp.int32, (BLK, LANES), 1)
    first = lane < HEAD_DIM
    units = [(i, p, hh) for i in range(nblk) for p in range(4) for hh in range(2)]

    def scores(u):
        i, p, hh = u
        qp = get_q(i, p)
        qm = jnp.where(first if hh == 0 else ~first, qp, jnp.zeros_like(qp))
        return lax.dot_general(qm, get_k(i, p), (((1,), (1,)), ((), ())), preferred_element_type=F32)

    def finish(u, s):
        i, p, hh = u
        sink = get_sink(p, hh)
        s = jnp.where(get_valid(i), s, NEG_INF)
        m = jnp.max(s, axis=-1, keepdims=True)
        if sink is not None:
            m = jnp.maximum(m, sink)
        pr = jnp.exp(s - m)
        den = jnp.sum(pr, axis=-1, keepdims=True)
        if sink is not None:
            den = den + jnp.exp(sink - m)
        return jnp.dot(pr.astype(BF16), get_v(i, p), preferred_element_type=F32), m, den

    pending = {k: scores(units[k]) for k in range(min(ATT_LOOKAHEAD, len(units)))}
    done = {}
    for k, u in enumerate(units):
        if k + ATT_LOOKAHEAD < len(units):
            pending[k + ATT_LOOKAHEAD] = scores(units[k + ATT_LOOKAHEAD])
        done[u] = finish(u, pending.pop(k))
        i, p, hh = u
        if hh == 1:
            (o0, m0, d0), (o1, m1, d1) = done.pop((i, p, 0)), done.pop((i, p, 1))
            den = jnp.where(first, d0, d1)
            emit(i, p, jnp.where(first, o0, o1) / den, jnp.where(first, m0, m1) + jnp.log(den))


def _attn_a_kernel(q_ref, kc_ref, kp_ref, vc_ref, vp_ref, o_ref, lse_ref, *, max_dist):
    first_tile = pl.program_id(2) == 0
    lanes = lambda p: slice(p * LANES, (p + 1) * LANES)
    rows = lambda i: slice(i * BLK, (i + 1) * BLK)
    valid = {}

    def get_valid(i):
        key = min(i, 1)
        if key not in valid:
            valid[key] = _band_valid(max_dist, first_tile if i == 0 else None)
        return valid[key]

    def emit(i, p, o, lse):
        o_ref[0, rows(i), lanes(p)] = o.astype(BF16)
        lse_ref[0, rows(i), lanes(p)] = lse

    _attention_tile(q_ref.shape[1] // BLK,
                    lambda i, p: q_ref[0, rows(i), lanes(p)],
                    lambda i, p: _kv_window(kc_ref, kp_ref, i, lanes(p)),
                    lambda i, p: _kv_window(vc_ref, vp_ref, i, lanes(p)),
                    get_valid, lambda p, hh: None, emit)


def _attn_a(ua_d, bsz, seq, window, dil):
    l = seq // dil
    steps = window // dil
    view = ua_d.reshape(bsz, l, dil * UA_W)
    qt = min(ATT_QT, l)
    blk = (1, qt, 512)
    pblk = (1, BLK, 512)
    cur = lambda c: (lambda b, r, j: (b, j, 3 * r + c))
    prev = lambda c: (lambda b, r, j: (b, jnp.maximum(j * (qt // BLK) - 1, 0), 3 * r + c))
    out = lambda b, r, j: (b, j, r)
    o, lse = pl.pallas_call(
        functools.partial(_attn_a_kernel, max_dist=steps),
        grid=(bsz, dil, l // qt),
        in_specs=[pl.BlockSpec(blk, cur(0)), pl.BlockSpec(blk, cur(1)), pl.BlockSpec(pblk, prev(1)),
                  pl.BlockSpec(blk, cur(2)), pl.BlockSpec(pblk, prev(2))],
        out_specs=[pl.BlockSpec(blk, out), pl.BlockSpec(blk, out)],
        out_shape=[jax.ShapeDtypeStruct((bsz, l, dil * 512), BF16),
                   jax.ShapeDtypeStruct((bsz, l, dil * 512), F32)],
        compiler_params=_cparams(("parallel", "parallel", "arbitrary")),
        name=f"attn_a_d{dil}",
    )(view, view, view, view, view)
    return o.reshape(bsz * l, dil * 512), lse.reshape(bsz * l, dil * 512)


def _attn_b_kernel(sink_ref, q_ref, kc_ref, kp_ref, vc_ref, vp_ref, o_ref):
    first_tile = pl.program_id(1) == 0
    lane = lax.broadcasted_iota(jnp.int32, (2 * BLK, LANES), 1)
    first = lane < HEAD_DIM
    lanes = lambda p: slice(p * LANES, (p + 1) * LANES)
    rows = lambda i: slice(i * BLK, (i + 1) * BLK)
    valid, dup = {}, {}

    def get_valid(i):
        key = min(i, 1)
        if key not in valid:
            valid[key] = _band_valid(B_WINDOW - 1, first_tile if i == 0 else None)
        return valid[key]

    def get_dup(cur_ref, prev_ref, i, kv):
        key = (id(cur_ref), i)
        if key not in dup:
            t = _kv_window(cur_ref, prev_ref, i, slice(0, LANES))
            t_sw = pltpu.roll(t, HEAD_DIM, 1)
            dup[key] = (jnp.where(first, t, t_sw), jnp.where(first, t_sw, t))
        return dup[key][kv]

    def emit(i, p, o, lse):
        o_ref[0, rows(i), lanes(p)] = o.astype(BF16)

    _attention_tile(q_ref.shape[1] // BLK,
                    lambda i, p: q_ref[0, rows(i), lanes(p)],
                    lambda i, p: get_dup(kc_ref, kp_ref, i, p // 2),
                    lambda i, p: get_dup(vc_ref, vp_ref, i, p // 2),
                    get_valid, lambda p, hh: sink_ref[2 * p + hh], emit)


def _attn_b(ub, sinks, bsz, seq):
    view = ub.reshape(bsz, seq, UB_W)
    qt = ATT_QT
    cur = lambda c: (lambda b, j: (b, j, c))
    prev = lambda c: (lambda b, j: (b, jnp.maximum(j * (qt // BLK) - 1, 0), c))
    o = pl.pallas_call(
        _attn_b_kernel,
        grid=(bsz, seq // qt),
        in_specs=[pl.BlockSpec(memory_space=pltpu.SMEM),
                  pl.BlockSpec((1, qt, 512), lambda b, j: (b, j, 0)),
                  pl.BlockSpec((1, qt, LANES), cur(4)), pl.BlockSpec((1, BLK, LANES), prev(4)),
                  pl.BlockSpec((1, qt, LANES), cur(5)), pl.BlockSpec((1, BLK, LANES), prev(5))],
        out_specs=pl.BlockSpec((1, qt, 512), lambda b, j: (b, j, 0)),
        out_shape=jax.ShapeDtypeStruct((bsz, seq, 512), BF16),
        compiler_params=_cparams(("parallel", "arbitrary")),
        name="attn_b",
    )(sinks, view, view, view, view, view)
    return o.reshape(bsz * seq, 512)


def _split3(a):
    hi = a.astype(BF16)
    r = a - hi.astype(F32)
    mid = r.astype(BF16)
    lo = (r - mid.astype(F32)).astype(BF16)
    return hi, mid, lo


def _dot_nt(a, b):
    return lax.dot_general(a, b, (((1,), (1,)), ((), ())), preferred_element_type=F32)


def _dot_tn(a, b):
    return lax.dot_general(a, b, (((0,), (0,)), ((), ())), preferred_element_type=F32)


def _mm(a, b):
    return jnp.dot(a.astype(BF16), b.astype(BF16), preferred_element_type=F32)


GDN_TT = 256
GDN_NCH = GDN_TT // C_CHUNK
GDN_CPI = 2
GDN_NBAT = 4


def _gdn_kernel(cq_ref, ck_ref, cv_ref, zs_ref, sm_ref, alog_ref, dtb_ref, cn_ref, e_ref, lbd_ref,
                y_ref, state_ref, expd_ref, u_ref, w_ref, qd_ref, kd_ref, attn_ref):
    c, tt = C_CHUNK, GDN_TT

    @pl.when(pl.program_id(1) == 0)
    def _():
        state_ref[...] = jnp.zeros_like(state_ref)

    nbat = sm_ref.shape[0]
    lane = lax.broadcasted_iota(jnp.int32, (tt, LANES), 1)
    lbd = lbd_ref[...]
    for nb in range(nbat):
        sm = sm_ref[nb]
        z = sm + dtb_ref[...]
        softplus = jnp.maximum(z, 0.0) + jnp.log1p(jnp.exp(-jnp.abs(z)))
        g = -jnp.exp(alog_ref[...]) * softplus
        beta = jax.nn.sigmoid(sm)
        g_hi, g_mid, g_lo = _split3(g)
        gc = (jnp.dot(lbd, g_hi, preferred_element_type=F32) + jnp.dot(lbd, g_mid, preferred_element_type=F32)
              + jnp.dot(lbd, g_lo, preferred_element_type=F32))
        c_hi, c_mid, c_lo = _split3(jnp.where(lane < C_V_HEADS, gc, beta))
        pieces = jnp.where(lane < 16, c_hi,
                           jnp.where(lane < 32, pltpu.roll(c_mid, 16, 1),
                                     jnp.where(lane < 48, pltpu.roll(c_lo, 32, 1), jnp.zeros_like(c_hi))))
        for s in range(4):
            sl = slice(s * 512, (s + 1) * 512)
            expd_ref[nb, :, sl] = jnp.dot(pieces, e_ref[:, sl], preferred_element_type=F32)

    ri = lax.broadcasted_iota(jnp.int32, (c, LANES), 0)
    lj = lax.broadcasted_iota(jnp.int32, (c, LANES), 1)
    cj = lj & (c - 1)
    left = lj < c
    tril = ri >= cj
    strict = ri > cj
    eye = (ri == cj).astype(F32)
    bi = lax.broadcasted_iota(jnp.int32, (2 * c, LANES), 0)
    bj = lax.broadcasted_iota(jnp.int32, (2 * c, LANES), 1)
    blockdiag = (bi < c) == (bj < c)

    def bd(m2):
        m16 = m2.astype(BF16)
        return jnp.where(blockdiag, jnp.concatenate([m16, m16], axis=0), jnp.zeros((2 * c, LANES), BF16))

    def chunk_local(i, carry):
        units = []
        for nb, j in [(nb, j) for nb in range(nbat) for j in range(cpi)]:
            r0 = pl.multiple_of((i * cpi + j) * c, c)
            rows = pl.ds(r0, c)
            for p in range(C_V_HEADS // 2):
                hs = [slice(h * LANES, (h + 1) * LANES) for h in (2 * p, 2 * p + 1)]
                ps = slice(p * C_DK, (p + 1) * C_DK)
                units.append(dict(nb=nb, rows=rows, p=p, hs=hs, q16=cq_ref[nb, rows, ps], k16=ck_ref[nb, rows, ps],
                                  v=[cv_ref[nb, rows, s].astype(F32) for s in hs],
                                  ge=[expd_ref[nb, rows, s] for s in hs],
                                  be=[expd_ref[nb, rows, 1024 + s.start:1024 + s.stop] for s in hs]))
        for un in units:
            q16, k16 = un["q16"], un["k16"]
            un["qn"], un["kn"] = q16.astype(F32), k16.astype(F32)
            un["kq"] = _dot_nt(jnp.concatenate([k16, q16], axis=0),
                               jnp.concatenate([k16, k16], axis=0))
        for un in units:
            ge, be, kq = un["ge"], un["be"], un["kq"]
            grow = jnp.concatenate(ge, axis=0).T[0:c, :]
            diff = jnp.where(left, ge[0], ge[1]) - grow
            dec = jnp.where(tril, jnp.exp(jnp.where(tril, diff, 0.0)), 0.0)
            un["pw"] = jnp.where(strict, kq[0:c] * dec * jnp.where(left, be[0], be[1]), 0.0) * -1.0
            un["attn"] = jnp.where(tril, kq[c:] * dec, 0.0).astype(BF16)
            un["x"] = eye + un["pw"]
        for un in units:
            un["pw"] = _mm(un["pw"], bd(un["pw"]))
        for _ in range(4):
            for un in units:
                r = _mm(jnp.concatenate([un["pw"], un["x"]], axis=0), bd(un["pw"]))
                un["x"] = un["x"] + r[c:]
                un["pw"] = r[0:c]
        for un in units:
            un["x"] = un["x"] + _mm(un["x"], bd(un["pw"]))
        zero = jnp.zeros((c, 2 * C_DV), F32)
        for un in units:
            rhs, un["qd"], un["kd"] = [], [], []
            for ge, be, v in zip(un["ge"], un["be"], un["v"]):
                eg = jnp.exp(ge)
                rhs.append(jnp.concatenate([v * be, un["kn"] * be * eg], axis=1))
                un["qd"].append((un["qn"] * eg).astype(BF16))
                un["kd"].append((un["kn"] * jnp.exp(ge[c - 1:c, :] - ge)).astype(BF16))
            bdr = jnp.concatenate([jnp.concatenate([rhs[0], zero], axis=1),
                                   jnp.concatenate([zero, rhs[1]], axis=1)], axis=0)
            un["uw"] = _mm(un["x"], bdr)
        for un in units:
            nb, rows, p = un["nb"], un["rows"], un["p"]
            for t, s in enumerate(un["hs"]):
                u_ref[nb, rows, s] = un["uw"][:, 2 * t * C_DV:(2 * t + 1) * C_DV]
                w_ref[nb, rows, s] = un["uw"][:, (2 * t + 1) * C_DV:(2 * t + 2) * C_DV].astype(BF16)
                qd_ref[nb, rows, s] = un["qd"][t]
                kd_ref[nb, rows, s] = un["kd"][t]
            attn_ref[nb, rows, p * LANES:(p + 1) * LANES] = un["attn"]
        return carry

    cpi = max(1, GDN_CPI // nbat)
    lax.fori_loop(0, GDN_NCH // cpi, chunk_local, 0)
    cn = cn_ref[...]

    def recur(i, carry):
        r0 = pl.multiple_of(i * c, c)
        rows = pl.ds(r0, c)
        last8 = pl.ds(pl.multiple_of(r0 + c - 8, 8), 8)
        hsl = [slice(h * C_DV, (h + 1) * C_DV) for h in range(C_V_HEADS)]
        chains = [(nb, h) for nb in range(nbat) for h in range(C_V_HEADS)]
        s = {k: state_ref[k[0] * C_V_HEADS + k[1]] for k in chains}
        wq = {(nb, h): jnp.dot(jnp.concatenate([w_ref[nb, rows, hsl[h]], qd_ref[nb, rows, hsl[h]]], axis=0),
                               s[nb, h].astype(BF16), preferred_element_type=F32) for nb, h in chains}
        v16 = {(nb, h): (u_ref[nb, rows, hsl[h]] - wq[nb, h][0:c]).astype(BF16) for nb, h in chains}
        s_new = {(nb, h): s[nb, h] * jnp.exp(expd_ref[nb, last8, hsl[h]][7:8]) + _dot_tn(kd_ref[nb, rows, hsl[h]], v16[nb, h])
                 for nb, h in chains}
        zero = jnp.zeros((c, C_DV), BF16)
        o = {}
        for nb in range(nbat):
            for p in range(C_V_HEADS // 2):
                bdv = jnp.concatenate([jnp.concatenate([v16[nb, 2 * p], zero], axis=1),
                                       jnp.concatenate([zero, v16[nb, 2 * p + 1]], axis=1)], axis=0)
                av = jnp.dot(attn_ref[nb, rows, p * LANES:(p + 1) * LANES], bdv, preferred_element_type=F32)
                o[nb, 2 * p] = wq[nb, 2 * p][c:] + av[:, 0:C_DV]
                o[nb, 2 * p + 1] = wq[nb, 2 * p + 1][c:] + av[:, C_DV:]
        for nb, h in chains:
            state_ref[nb * C_V_HEADS + h] = s_new[nb, h]
            y_ref[nb, rows, hsl[h]] = (_rms(o[nb, h], cn) * zs_ref[nb, rows, hsl[h]].astype(F32)).astype(BF16)
        return carry

    lax.fori_loop(0, GDN_NCH, recur, 0)


def _gdn(uc, sm, a_log, dt_bias, c_norm, bsz, seq):
    c, tt = C_CHUNK, GDN_TT
    nt = seq // tt
    pad = lambda v: jnp.pad(v.astype(F32), (0, LANES - v.shape[0])).reshape(1, LANES)
    li = np.arange(LANES)[:, None]
    cj = np.arange(2048)[None, :]
    lp = li % 16
    e = jnp.asarray((li < 48) & (((lp < 8) & (cj < 1024) & (cj // LANES == lp))
                                 | ((lp >= 8) & (cj >= 1024) & ((cj - 1024) // LANES == lp - 8))), BF16)
    ti = np.arange(tt)
    lbd = jnp.asarray((ti[:, None] // c == ti[None, :] // c) & (ti[:, None] >= ti[None, :]), BF16)
    nbat = GDN_NBAT if bsz % GDN_NBAT == 0 else 1
    uc3 = uc.reshape(bsz, seq, UC_W)
    sm3 = sm.reshape(bsz, seq, SM_W)
    row = lambda blockcol: (lambda b, t: (b, t, blockcol))
    const = lambda b, t: (0, 0)
    yc = pl.pallas_call(
        _gdn_kernel,
        grid=(bsz // nbat, nt),
        in_specs=[
            pl.BlockSpec((nbat, tt, 512), row(0)),
            pl.BlockSpec((nbat, tt, 512), row(1)),
            pl.BlockSpec((nbat, tt, 1024), row(1)),
            pl.BlockSpec((nbat, tt, 1024), row(2)),
            pl.BlockSpec((nbat, tt, SM_W), row(0)),
            pl.BlockSpec((1, LANES), const),
            pl.BlockSpec((1, LANES), const),
            pl.BlockSpec((1, LANES), const),
            pl.BlockSpec((LANES, 2048), const),
            pl.BlockSpec((tt, tt), const),
        ],
        out_specs=pl.BlockSpec((nbat, tt, 1024), row(0)),
        out_shape=jax.ShapeDtypeStruct((bsz, seq, 1024), BF16),
        scratch_shapes=[
            pltpu.VMEM((nbat * C_V_HEADS, C_DK, C_DV), F32),
            pltpu.VMEM((nbat, tt, 2048), F32),
            pltpu.VMEM((nbat, tt, 1024), F32),
            pltpu.VMEM((nbat, tt, 1024), BF16),
            pltpu.VMEM((nbat, tt, 1024), BF16),
            pltpu.VMEM((nbat, tt, 1024), BF16),
            pltpu.VMEM((nbat, tt, 512), BF16),
        ],
        compiler_params=_cparams(("parallel", "arbitrary")),
        name="gdn",
    )(uc3, uc3, uc3, uc3, sm3, pad(a_log), pad(dt_bias), c_norm.astype(F32).reshape(1, LANES), e, lbd)
    return yc.reshape(bsz * seq, 1024)


def _merge_kernel(x_ref, o1_ref, o2_ref, o3_ref, l1_ref, l2_ref, l3_ref, yb_ref, yc_ref, ug_ref,
                  wa_ref, wb_ref, wc_ref, wo_ref, out_ref, nat_ref):
    tm = x_ref.shape[0]
    for slot, (dil, src) in enumerate(((4, o2_ref), (4, l2_ref), (16, o3_ref), (16, l3_ref))):
        for r in range(dil):
            for g in range(4):
                col = r * 512 + g * LANES
                nat_ref[slot, g, pl.ds(r, tm // dil, stride=dil), :] = src[:, col:col + LANES].astype(F32)
    parts = []
    for g in range(4):
        sl = slice(g * LANES, (g + 1) * LANES)
        l1, l2, l3 = l1_ref[:, sl], nat_ref[1, g], nat_ref[3, g]
        m = jnp.maximum(jnp.maximum(l1, l2), l3)
        e1, e2, e3 = jnp.exp(l1 - m), jnp.exp(l2 - m), jnp.exp(l3 - m)
        parts.append(((e1 * o1_ref[:, sl].astype(F32) + e2 * nat_ref[0, g] + e3 * nat_ref[2, g])
                      / (e1 + e2 + e3)).astype(BF16))
    ya = jnp.concatenate(parts, axis=1)
    merged = (ug_ref[:, 0:1024].astype(F32) * jnp.dot(ya, wa_ref[...], preferred_element_type=F32)
              + ug_ref[:, 1024:2048].astype(F32) * jnp.dot(yb_ref[...], wb_ref[...], preferred_element_type=F32)
              + ug_ref[:, 2048:3072].astype(F32) * jnp.dot(yc_ref[...], wc_ref[...], preferred_element_type=F32))
    out_ref[...] = x_ref[...] + jnp.dot(merged.astype(BF16), wo_ref[...], preferred_element_type=F32)


def _merge(x2, o_list, lse_list, yb, yc, ug, wa, wb, wc, wo, tm):
    m = x2.shape[0]
    row = lambda i: (i, 0)
    const = lambda i: (0, 0)
    rs = lambda w, dil=1: pl.BlockSpec((tm // dil, dil * w), row)
    return pl.pallas_call(
        _merge_kernel,
        grid=(m // tm,),
        in_specs=[rs(D_MODEL), rs(512), rs(512, 4), rs(512, 16), rs(512), rs(512, 4), rs(512, 16),
                  rs(512), rs(1024), rs(UG_W),
                  pl.BlockSpec((512, D_MODEL), const), pl.BlockSpec((512, D_MODEL), const),
                  pl.BlockSpec((1024, D_MODEL), const), pl.BlockSpec((D_MODEL, D_MODEL), const)],
        out_specs=rs(D_MODEL),
        out_shape=jax.ShapeDtypeStruct((m, D_MODEL), F32),
        scratch_shapes=[pltpu.VMEM((4, 4, tm, LANES), F32)],
        compiler_params=_cparams(("parallel",)),
        name="merge",
    )(x2, *o_list, *lse_list, yb, yc, ug, wa, wb, wc, wo)


def _ffn_kernel(x_ref, g_ref, w1_ref, w2_ref, gf_ref, out_ref, *, final_norm):
    x = x_ref[...]
    h = _rms(x, g_ref[...]).astype(BF16)
    acc = x
    for c in range(D_FF // 1024):
        sl = slice(c * 1024, (c + 1) * 1024)
        a = jnp.maximum(jnp.dot(h, w1_ref[:, sl], preferred_element_type=F32), 0.0)
        acc = acc + jnp.dot((a * a).astype(BF16), w2_ref[sl, :], preferred_element_type=F32)
    out_ref[...] = _rms(acc, gf_ref[...]) if final_norm else acc


def _ffn(x2, gain, w1, w2, gain_final, final_norm, tm):
    m = x2.shape[0]
    row = lambda i: (i, 0)
    const = lambda i: (0, 0)
    return pl.pallas_call(
        functools.partial(_ffn_kernel, final_norm=final_norm),
        grid=(m // tm,),
        in_specs=[pl.BlockSpec((tm, D_MODEL), row), pl.BlockSpec((1, D_MODEL), const),
                  pl.BlockSpec((D_MODEL, D_FF), const), pl.BlockSpec((D_FF, D_MODEL), const),
                  pl.BlockSpec((1, D_MODEL), const)],
        out_specs=pl.BlockSpec((tm, D_MODEL), row),
        out_shape=jax.ShapeDtypeStruct((m, D_MODEL), F32),
        compiler_params=_cparams(("parallel",)),
        name="ffn",
    )(x2, gain, w1, w2, gain_final)


def _rope_tables(positions):
    half = ROT_DIM // 2
    inv_freq = jnp.power(ROPE_THETA, -jnp.arange(0, ROT_DIM, 2, dtype=F32) / ROT_DIM)
    rest = jnp.zeros((HEAD_DIM - ROT_DIM,), F32)
    zero8 = jnp.zeros((half,), F32)
    two = lambda t: jnp.concatenate([t, t])
    freq = two(jnp.concatenate([inv_freq, inv_freq, rest]))
    m1 = two(jnp.concatenate([-jnp.ones((half,), F32), zero8, rest]))
    m2 = two(jnp.concatenate([zero8, jnp.ones((half,), F32), rest]))
    ang = positions.astype(F32).reshape(-1, 1) * freq
    sin = jnp.sin(ang)
    return jnp.cos(ang), sin * m1, sin * m2


def kernel(x, positions, norm_mix, w_in, b_in, conv_w, a_log, dt_bias, sinks, c_norm, w_branch_a, w_branch_b,
           w_branch_c, w_out, norm_ffn, w_ff1, w_ff2, norm_final):
    bsz, seq, d = x.shape
    depth = w_in.shape[0]
    assert d == D_MODEL and seq % (16 * BLK) == 0
    m = bsz * seq
    cosv, s1, s2 = _rope_tables(positions)
    x2 = x.reshape(m, d)
    ca = 5376
    for layer in range(depth):
        wl, bl = w_in[layer], b_in[layer]
        w_main = wl[:, :ca].astype(BF16)
        w_gate = wl[:, ca + 16:].astype(BF16)
        w_small = jnp.pad(wl[:, ca:ca + 16], ((0, 0), (0, SM_W - 16))).astype(BF16)
        b_all = jnp.concatenate([bl[:ca], bl[ca + 16:], bl[ca:ca + 16], jnp.zeros((SM_W - 16,), F32)]).reshape(1, -1)
        ua, ua4, ua16, ub, uc, ug, sm = _inproj(x2, norm_mix[layer].reshape(1, d), w_main, w_gate, w_small, b_all,
                                                cosv, s1, s2, conv_w[layer].astype(F32), seq, tm=512)
        o_list, lse_list = [], []
        for (window, dil), ua_d in zip(A_CONFIGS, (ua, ua4, ua16)):
            o, lse = _attn_a(ua_d, bsz, seq, window, dil)
            o_list.append(o)
            lse_list.append(lse)
        yb = _attn_b(ub, sinks[layer].astype(F32), bsz, seq)
        yc = _gdn(uc, sm, a_log[layer], dt_bias[layer], c_norm[layer], bsz, seq)
        x2 = _merge(x2, o_list, lse_list, yb, yc, ug, w_branch_a[layer].astype(BF16), w_branch_b[layer].astype(BF16),
                    w_branch_c[layer].astype(BF16), w_out[layer].astype(BF16), tm=512)
        x2 = _ffn(x2, norm_ffn[layer].reshape(1, d), w_ff1[layer].astype(BF16), w_ff2[layer].astype(BF16),
                  norm_final.reshape(1, d), final_norm=(layer == depth - 1), tm=512)
    return x2.reshape(bsz, seq, d)
```

```python
import functools
import math

import jax
import jax.numpy as jnp
import numpy as np
from jax import lax
from jax.experimental import pallas as pl
from jax.experimental.pallas import tpu as pltpu

F32 = jnp.float32
BF16 = jnp.bfloat16

D_MODEL = 1024
HEAD_DIM = 64
ROT_DIM = 16
ROPE_THETA = 500000.0
BLK = 128
NEG_INF = -1e30
EPS = 1e-6
A_CONFIGS = ((128, 1), (512, 4), (2048, 16))
B_WINDOW = 128
C_V_HEADS = 8
C_DK = 128
C_DV = 128
C_CONV = 4
C_CHUNK = 64
D_FF = 4096

LANES = 128
UA_W = 1536
UB_W = 768
UC_W = 3072
UG_W = 3072
SM_W = 128
W_ALL = UA_W + UB_W + UC_W + UG_W + SM_W
VMEM_LIMIT = 56 * 1024 * 1024


def _cparams(sem):
    return pltpu.CompilerParams(dimension_semantics=sem, vmem_limit_bytes=VMEM_LIMIT)


def _rms(x, gain):
    return x * lax.rsqrt(jnp.mean(x * x, axis=-1, keepdims=True) + EPS) * gain


def _inproj_kernel(x_ref, g_ref, wm_ref, wg_ref, ws_ref, b_ref, cos_ref, s1_ref, s2_ref, cw_ref,
                   ua_ref, ua4_ref, ua16_ref, ub_ref, uc_ref, ug_ref, sm_ref, stage_ref, ext_ref, h_ref, stage4_ref,
                   *, tiles_per_seq):
    h_ref[...] = _rms(x_ref[...], g_ref[...]).astype(BF16)
    cosv, s1, s2 = cos_ref[...], s1_ref[...], s2_ref[...]
    w_main = UA_W + UB_W + UC_W

    def seg(off, width):
        if off < w_main:
            w = wm_ref[:, off:off + width]
        elif off < w_main + UG_W:
            w = wg_ref[:, off - w_main:off - w_main + width]
        else:
            w = ws_ref[...]
        return jnp.dot(h_ref[...], w, preferred_element_type=F32) + b_ref[:, off:off + width]

    def rope(a, scale):
        parts = []
        for g in range(a.shape[1] // LANES):
            t = a[:, g * LANES:(g + 1) * LANES]
            r = t * cosv + pltpu.roll(t, LANES - 8, 1) * s1 + pltpu.roll(t, 8, 1) * s2
            parts.append(r * scale if scale != 1.0 else r)
        return jnp.concatenate(parts, axis=1)

    qscale = HEAD_DIM ** -0.5
    tm = x_ref.shape[0]

    def seg_a(c):
        val = seg(c * 512, 512)
        if c < 2:
            val = rope(val, qscale if c == 0 else 1.0)
        ua_ref[:, c * 512:(c + 1) * 512] = val.astype(BF16)
        for gl in range(4):
            g = 4 * c + gl
            stage_ref[g] = val[:, gl * LANES:(gl + 1) * LANES]
            for b in range(4):
                rows4 = stage_ref[g, pl.ds(b, tm // 4, stride=4), :]
                ua4_ref[:, b * UA_W + g * LANES:b * UA_W + (g + 1) * LANES] = rows4.astype(BF16)
                stage4_ref[g % 2, b] = rows4
                for a in range(4):
                    col = (4 * a + b) * UA_W + g * LANES
                    ua16_ref[:, col:col + LANES] = stage4_ref[g % 2, b, pl.ds(a, tm // 16, stride=4), :].astype(BF16)

    def seg_b():
        ub_ref[:, 0:512] = rope(seg(UA_W, 512), qscale).astype(BF16)
        ub_ref[:, 512:640] = rope(seg(UA_W + 512, 128), 1.0).astype(BF16)
        ub_ref[:, 640:768] = seg(UA_W + 640, 128).astype(BF16)

    def silu(t):
        return t * (0.5 + 0.5 * jnp.tanh(0.5 * t))

    @pl.when(pl.program_id(0) % tiles_per_seq == 0)
    def _():
        ext_ref[:, 0:8, :] = jnp.zeros((ext_ref.shape[0], 8, LANES), F32)

    def seg_conv(c):
        val = seg(UA_W + UB_W + c * 512, 512)
        for gl in range(4):
            g = 4 * c + gl
            sl = slice(g * LANES, (g + 1) * LANES)
            cur = val[:, gl * LANES:(gl + 1) * LANES]
            ext_ref[g, 8:8 + tm, :] = cur
            y = cur * cw_ref[C_CONV - 1:C_CONV, sl]
            for j in range(C_CONV - 1):
                y = y + ext_ref[g, 5 + j:5 + j + tm, :] * cw_ref[j:j + 1, sl]
            ext_ref[g, 0:8, :] = cur[tm - 8:tm]
            y = silu(y)
            if c < 2:
                scale = C_DK ** -0.5 if c == 0 else 1.0
                y = y * (lax.rsqrt(jnp.sum(y * y, axis=-1, keepdims=True) + EPS) * scale)
            uc_ref[:, sl] = y.astype(BF16)

    def seg_z(c):
        uc_ref[:, c * 512:(c + 1) * 512] = silu(seg(UA_W + UB_W + c * 512, 512)).astype(BF16)

    def seg_gate(c):
        ug_ref[:, c * 256:(c + 1) * 256] = jax.nn.sigmoid(seg(UA_W + UB_W + UC_W + c * 256, 256)).astype(BF16)

    def seg_small():
        sm_ref[...] = seg(UA_W + UB_W + UC_W + UG_W, SM_W)

    heavy = ([functools.partial(seg_a, c) for c in range(3)] + [seg_b]
             + [functools.partial(seg_conv, c) for c in range(4)]
             + [functools.partial(seg_z, c) for c in (4, 5)] + [seg_small])
    gates = [functools.partial(seg_gate, c) for c in range(12)]
    for k, fn in enumerate(heavy):
        fn()
        for gfn in gates[k * len(gates) // len(heavy):(k + 1) * len(gates) // len(heavy)]:
            gfn()


def _inproj(x2, gain, w_main, w_gate, w_small, b_all, cosv, s1, s2, conv_w, seq, tm):
    m = x2.shape[0]
    row = lambda i: (i, 0)
    const = lambda i: (0, 0)
    return pl.pallas_call(
        functools.partial(_inproj_kernel, tiles_per_seq=seq // tm),
        grid=(m // tm,),
        in_specs=[
            pl.BlockSpec((tm, D_MODEL), row),
            pl.BlockSpec((1, D_MODEL), const),
            pl.BlockSpec((D_MODEL, UA_W + UB_W + UC_W), const, pipeline_mode=pl.Buffered(1)),
            pl.BlockSpec((D_MODEL, UG_W), const, pipeline_mode=pl.Buffered(1)),
            pl.BlockSpec((D_MODEL, SM_W), const, pipeline_mode=pl.Buffered(1)),
            pl.BlockSpec((1, W_ALL), const),
            pl.BlockSpec((tm, LANES), row),
            pl.BlockSpec((tm, LANES), row),
            pl.BlockSpec((tm, LANES), row),
            pl.BlockSpec((C_CONV, 2048), const),
        ],
        out_specs=[
            pl.BlockSpec((tm, UA_W), row),
            pl.BlockSpec((tm // 4, 4 * UA_W), row),
            pl.BlockSpec((tm // 16, 16 * UA_W), row),
            pl.BlockSpec((tm, UB_W), row),
            pl.BlockSpec((tm, UC_W), row),
            pl.BlockSpec((tm, UG_W), row),
            pl.BlockSpec((tm, SM_W), row),
        ],
        out_shape=[
            jax.ShapeDtypeStruct((m, UA_W), BF16),
            jax.ShapeDtypeStruct((m // 4, 4 * UA_W), BF16),
            jax.ShapeDtypeStruct((m // 16, 16 * UA_W), BF16),
            jax.ShapeDtypeStruct((m, UB_W), BF16),
            jax.ShapeDtypeStruct((m, UC_W), BF16),
            jax.ShapeDtypeStruct((m, UG_W), BF16),
            jax.ShapeDtypeStruct((m, SM_W), F32),
        ],
        scratch_shapes=[pltpu.VMEM((UA_W // LANES, tm, LANES), F32),
                        pltpu.VMEM((2048 // LANES, tm + 8, LANES), F32),
                        pltpu.VMEM((tm, D_MODEL), BF16),
                        pltpu.VMEM((2, 4, tm // 4, LANES), F32)],
        compiler_params=_cparams(("arbitrary",)),
        name="inproj",
    )(x2, gain, w_main, w_gate, w_small, b_all, cosv, s1, s2, conv_w)


ATT_QT = 512


def _band_valid(max_dist, first_block):
    qi = lax.broadcasted_iota(jnp.int32, (BLK, 2 * BLK), 0)
    kj = lax.broadcasted_iota(jnp.int32, (BLK, 2 * BLK), 1)
    dist = BLK + qi - kj
    band = (dist >= 0) & (dist <= max_dist)
    if first_block is None:
        return band
    return band & (jnp.logical_not(first_block) | (kj >= BLK))


def _kv_window(cur_ref, prev_ref, i, sl):
    if i == 0:
        return jnp.concatenate([prev_ref[0, :, sl], cur_ref[0, 0:BLK, sl]], axis=0)
    return cur_ref[0, (i - 1) * BLK:(i + 1) * BLK, sl]


ATT_LOOKAHEAD = 2


def _attention_tile(nblk, get_q, get_k, get_v, get_valid, get_sink, emit):
    lane = lax.broadcasted_iota(jnp.int32, (BLK, LANES), 1)
    first = lane < HEAD_DIM
    units = [(i, p, hh) for i in range(nblk) for p in range(4) for hh in range(2)]

    def scores(u):
        i, p, hh = u
        qp = get_q(i, p)
        qm = jnp.where(first if hh == 0 else ~first, qp, jnp.zeros_like(qp))
        return lax.dot_general(qm, get_k(i, p), (((1,), (1,)), ((), ())), preferred_element_type=F32)

    def finish(u, s):
        i, p, hh = u
        sink = get_sink(p, hh)
        s = jnp.where(get_valid(i), s, NEG_INF)
        m = jnp.max(s, axis=-1, keepdims=True)
        if sink is not None:
            m = jnp.maximum(m, sink)
        pr = jnp.exp(s - m)
        den = jnp.sum(pr, axis=-1, keepdims=True)
        if sink is not None:
            den = den + jnp.exp(sink - m)
        return jnp.dot(pr.astype(BF16), get_v(i, p), preferred_element_type=F32), m, den

    pending = {k: scores(units[k]) for k in range(min(ATT_LOOKAHEAD, len(units)))}
    done = {}
    for k, u in enumerate(units):
        if k + ATT_LOOKAHEAD < len(units):
            pending[k + ATT_LOOKAHEAD] = scores(units[k + ATT_LOOKAHEAD])
        done[u] = finish(u, pending.pop(k))
        i, p, hh = u
        if hh == 1:
            (o0, m0, d0), (o1, m1, d1) = done.pop((i, p, 0)), done.pop((i, p, 1))
            den = jnp.where(first, d0, d1)
            emit(i, p, jnp.where(first, o0, o1) / den, jnp.where(first, m0, m1) + jnp.log(den))


def _attn_a_kernel(q_ref, kc_ref, kp_ref, vc_ref, vp_ref, o_ref, lse_ref, *, max_dist):
    first_tile = pl.program_id(2) == 0
    lanes = lambda p: slice(p * LANES, (p + 1) * LANES)
    rows = lambda i: slice(i * BLK, (i + 1) * BLK)
    valid = {}

    def get_valid(i):
        key = min(i, 1)
        if key not in valid:
            valid[key] = _band_valid(max_dist, first_tile if i == 0 else None)
        return valid[key]

    def emit(i, p, o, lse):
        o_ref[0, rows(i), lanes(p)] = o.astype(BF16)
        lse_ref[0, rows(i), lanes(p)] = lse

    _attention_tile(q_ref.shape[1] // BLK,
                    lambda i, p: q_ref[0, rows(i), lanes(p)],
                    lambda i, p: _kv_window(kc_ref, kp_ref, i, lanes(p)),
                    lambda i, p: _kv_window(vc_ref, vp_ref, i, lanes(p)),
                    get_valid, lambda p, hh: None, emit)


def _attn_a(ua_d, bsz, seq, window, dil):
    l = seq // dil
    steps = window // dil
    view = ua_d.reshape(bsz, l, dil * UA_W)
    qt = min(ATT_QT, l)
    blk = (1, qt, 512)
    pblk = (1, BLK, 512)
    cur = lambda c: (lambda b, r, j: (b, j, 3 * r + c))
    prev = lambda c: (lambda b, r, j: (b, jnp.maximum(j * (qt // BLK) - 1, 0), 3 * r + c))
    out = lambda b, r, j: (b, j, r)
    o, lse = pl.pallas_call(
        functools.partial(_attn_a_kernel, max_dist=steps),
        grid=(bsz, dil, l // qt),
        in_specs=[pl.BlockSpec(blk, cur(0)), pl.BlockSpec(blk, cur(1)), pl.BlockSpec(pblk, prev(1)),
                  pl.BlockSpec(blk, cur(2)), pl.BlockSpec(pblk, prev(2))],
        out_specs=[pl.BlockSpec(blk, out), pl.BlockSpec(blk, out)],
        out_shape=[jax.ShapeDtypeStruct((bsz, l, dil * 512), BF16),
                   jax.ShapeDtypeStruct((bsz, l, dil * 512), F32)],
        compiler_params=_cparams(("parallel", "parallel", "arbitrary")),
        name=f"attn_a_d{dil}",
    )(view, view, view, view, view)
    return o.reshape(bsz * l, dil * 512), lse.reshape(bsz * l, dil * 512)


def _attn_b_kernel(sink_ref, q_ref, kc_ref, kp_ref, vc_ref, vp_ref, o_ref):
    first_tile = pl.program_id(1) == 0
    lane = lax.broadcasted_iota(jnp.int32, (2 * BLK, LANES), 1)
    first = lane < HEAD_DIM
    lanes = lambda p: slice(p * LANES, (p + 1) * LANES)
    rows = lambda i: slice(i * BLK, (i + 1) * BLK)
    valid, dup = {}, {}

    def get_valid(i):
        key = min(i, 1)
        if key not in valid:
            valid[key] = _band_valid(B_WINDOW - 1, first_tile if i == 0 else None)
        return valid[key]

    def get_dup(cur_ref, prev_ref, i, kv):
        key = (id(cur_ref), i)
        if key not in dup:
            t = _kv_window(cur_ref, prev_ref, i, slice(0, LANES))
            t_sw = pltpu.roll(t, HEAD_DIM, 1)
            dup[key] = (jnp.where(first, t, t_sw), jnp.where(first, t_sw, t))
        return dup[key][kv]

    def emit(i, p, o, lse):
        o_ref[0, rows(i), lanes(p)] = o.astype(BF16)

    _attention_tile(q_ref.shape[1] // BLK,
                    lambda i, p: q_ref[0, rows(i), lanes(p)],
                    lambda i, p: get_dup(kc_ref, kp_ref, i, p // 2),
                    lambda i, p: get_dup(vc_ref, vp_ref, i, p // 2),
                    get_valid, lambda p, hh: sink_ref[2 * p + hh], emit)


def _attn_b(ub, sinks, bsz, seq):
    view = ub.reshape(bsz, seq, UB_W)
    qt = ATT_QT
    cur = lambda c: (lambda b, j: (b, j, c))
    prev = lambda c: (lambda b, j: (b, jnp.maximum(j * (qt // BLK) - 1, 0), c))
    o = pl.pallas_call(
        _attn_b_kernel,
        grid=(bsz, seq // qt),
        in_specs=[pl.BlockSpec(memory_space=pltpu.SMEM),
                  pl.BlockSpec((1, qt, 512), lambda b, j: (b, j, 0)),
                  pl.BlockSpec((1, qt, LANES), cur(4)), pl.BlockSpec((1, BLK, LANES), prev(4)),
                  pl.BlockSpec((1, qt, LANES), cur(5)), pl.BlockSpec((1, BLK, LANES), prev(5))],
        out_specs=pl.BlockSpec((1, qt, 512), lambda b, j: (b, j, 0)),
        out_shape=jax.ShapeDtypeStruct((bsz, seq, 512), BF16),
        compiler_params=_cparams(("parallel", "arbitrary")),
        name="attn_b",
    )(sinks, view, view, view, view, view)
    return o.reshape(bsz * seq, 512)


def _split3(a):
    hi = a.astype(BF16)
    r = a - hi.astype(F32)
    mid = r.astype(BF16)
    lo = (r - mid.astype(F32)).astype(BF16)
    return hi, mid, lo


def _dot_nt(a, b):
    return lax.dot_general(a, b, (((1,), (1,)), ((), ())), preferred_element_type=F32)


def _dot_tn(a, b):
    return lax.dot_general(a, b, (((0,), (0,)), ((), ())), preferred_element_type=F32)


def _mm(a, b):
    return jnp.dot(a.astype(BF16), b.astype(BF16), preferred_element_type=F32)


GDN_TT = 256
GDN_NCH = GDN_TT // C_CHUNK
GDN_CPI = 2
GDN_NBAT = 4


def _gdn_kernel(cq_ref, ck_ref, cv_ref, zs_ref, sm_ref, alog_ref, dtb_ref, cn_ref, e_ref, lbd_ref,
                y_ref, state_ref, expd_ref, u_ref, w_ref, qd_ref, kd_ref, attn_ref):
    c, tt = C_CHUNK, GDN_TT

    @pl.when(pl.program_id(1) == 0)
    def _():
        state_ref[...] = jnp.zeros_like(state_ref)

    nbat = sm_ref.shape[0]
    lane = lax.broadcasted_iota(jnp.int32, (tt, LANES), 1)
    lbd = lbd_ref[...]
    for nb in range(nbat):
        sm = sm_ref[nb]
        z = sm + dtb_ref[...]
        softplus = jnp.maximum(z, 0.0) + jnp.log1p(jnp.exp(-jnp.abs(z)))
        g = -jnp.exp(alog_ref[...]) * softplus
        beta = jax.nn.sigmoid(sm)
        g_hi, g_mid, g_lo = _split3(g)
        gc = (jnp.dot(lbd, g_hi, preferred_element_type=F32) + jnp.dot(lbd, g_mid, preferred_element_type=F32)
              + jnp.dot(lbd, g_lo, preferred_element_type=F32))
        c_hi, c_mid, c_lo = _split3(jnp.where(lane < C_V_HEADS, gc, beta))
        pieces = jnp.where(lane < 16, c_hi,
                           jnp.where(lane < 32, pltpu.roll(c_mid, 16, 1),
                                     jnp.where(lane < 48, pltpu.roll(c_lo, 32, 1), jnp.zeros_like(c_hi))))
        for s in range(4):
            sl = slice(s * 512, (s + 1) * 512)
            expd_ref[nb, :, sl] = jnp.dot(pieces, e_ref[:, sl], preferred_element_type=F32)

    ri = lax.broadcasted_iota(jnp.int32, (c, LANES), 0)
    lj = lax.broadcasted_iota(jnp.int32, (c, LANES), 1)
    cj = lj & (c - 1)
    left = lj < c
    tril = ri >= cj
    strict = ri > cj
    eye = (ri == cj).astype(F32)
    bi = lax.broadcasted_iota(jnp.int32, (2 * c, LANES), 0)
    bj = lax.broadcasted_iota(jnp.int32, (2 * c, LANES), 1)
    blockdiag = (bi < c) == (bj < c)

    def bd(m2):
        m16 = m2.astype(BF16)
        return jnp.where(blockdiag, jnp.concatenate([m16, m16], axis=0), jnp.zeros((2 * c, LANES), BF16))

    def chunk_local(i, carry):
        units = []
        for nb, j in [(nb, j) for nb in range(nbat) for j in range(cpi)]:
            r0 = pl.multiple_of((i * cpi + j) * c, c)
            rows = pl.ds(r0, c)
            for p in range(C_V_HEADS // 2):
                hs = [slice(h * LANES, (h + 1) * LANES) for h in (2 * p, 2 * p + 1)]
                ps = slice(p * C_DK, (p + 1) * C_DK)
                units.append(dict(nb=nb, rows=rows, p=p, hs=hs, q16=cq_ref[nb, rows, ps], k16=ck_ref[nb, rows, ps],
                                  v=[cv_ref[nb, rows, s].astype(F32) for s in hs],
                                  ge=[expd_ref[nb, rows, s] for s in hs],
                                  be=[expd_ref[nb, rows, 1024 + s.start:1024 + s.stop] for s in hs]))
        for un in units:
            q16, k16 = un["q16"], un["k16"]
            un["qn"], un["kn"] = q16.astype(F32), k16.astype(F32)
            un["kq"] = _dot_nt(jnp.concatenate([k16, q16], axis=0),
                               jnp.concatenate([k16, k16], axis=0))
        for un in units:
            ge, be, kq = un["ge"], un["be"], un["kq"]
            grow = jnp.concatenate(ge, axis=0).T[0:c, :]
            diff = jnp.where(left, ge[0], ge[1]) - grow
            dec = jnp.where(tril, jnp.exp(jnp.where(tril, diff, 0.0)), 0.0)
            un["pw"] = jnp.where(strict, kq[0:c] * dec * jnp.where(left, be[0], be[1]), 0.0) * -1.0
            un["attn"] = jnp.where(tril, kq[c:] * dec, 0.0).astype(BF16)
            un["x"] = eye + un["pw"]
        for un in units:
            un["pw"] = _mm(un["pw"], bd(un["pw"]))
        for _ in range(4):
            for un in units:
                r = _mm(jnp.concatenate([un["pw"], un["x"]], axis=0), bd(un["pw"]))
                un["x"] = un["x"] + r[c:]
                un["pw"] = r[0:c]
        for un in units:
            un["x"] = un["x"] + _mm(un["x"], bd(un["pw"]))
        zero = jnp.zeros((c, 2 * C_DV), F32)
        for un in units:
            rhs, un["qd"], un["kd"] = [], [], []
            for ge, be, v in zip(un["ge"], un["be"], un["v"]):
                eg = jnp.exp(ge)
                rhs.append(jnp.concatenate([v * be, un["kn"] * be * eg], axis=1))
                un["qd"].append((un["qn"] * eg).astype(BF16))
                un["kd"].append((un["kn"] * jnp.exp(ge[c - 1:c, :] - ge)).astype(BF16))
            bdr = jnp.concatenate([jnp.concatenate([rhs[0], zero], axis=1),
                                   jnp.concatenate([zero, rhs[1]], axis=1)], axis=0)
            un["uw"] = _mm(un["x"], bdr)
        for un in units:
            nb, rows, p = un["nb"], un["rows"], un["p"]
            for t, s in enumerate(un["hs"]):
                u_ref[nb, rows, s] = un["uw"][:, 2 * t * C_DV:(2 * t + 1) * C_DV]
                w_ref[nb, rows, s] = un["uw"][:, (2 * t + 1) * C_DV:(2 * t + 2) * C_DV].astype(BF16)
                qd_ref[nb, rows, s] = un["qd"][t]
                kd_ref[nb, rows, s] = un["kd"][t]
            attn_ref[nb, rows, p * LANES:(p + 1) * LANES] = un["attn"]
        return carry

    cpi = max(1, GDN_CPI // nbat)
    lax.fori_loop(0, GDN_NCH // cpi, chunk_local, 0)
    cn = cn_ref[...]

    def recur(i, carry):
        r0 = pl.multiple_of(i * c, c)
        rows = pl.ds(r0, c)
        last8 = pl.ds(pl.multiple_of(r0 + c - 8, 8), 8)
        hsl = [slice(h * C_DV, (h + 1) * C_DV) for h in range(C_V_HEADS)]
        chains = [(nb, h) for nb in range(nbat) for h in range(C_V_HEADS)]
        s = {k: state_ref[k[0] * C_V_HEADS + k[1]] for k in chains}
        wq = {(nb, h): jnp.dot(jnp.concatenate([w_ref[nb, rows, hsl[h]], qd_ref[nb, rows, hsl[h]]], axis=0),
                               s[nb, h].astype(BF16), preferred_element_type=F32) for nb, h in chains}
        v16 = {(nb, h): (u_ref[nb, rows, hsl[h]] - wq[nb, h][0:c]).astype(BF16) for nb, h in chains}
        s_new = {(nb, h): s[nb, h] * jnp.exp(expd_ref[nb, last8, hsl[h]][7:8]) + _dot_tn(kd_ref[nb, rows, hsl[h]], v16[nb, h])
                 for nb, h in chains}
        zero = jnp.zeros((c, C_DV), BF16)
        o = {}
        for nb in range(nbat):
            for p in range(C_V_HEADS // 2):
                bdv = jnp.concatenate([jnp.concatenate([v16[nb, 2 * p], zero], axis=1),
                                       jnp.concatenate([zero, v16[nb, 2 * p + 1]], axis=1)], axis=0)
                av = jnp.dot(attn_ref[nb, rows, p * LANES:(p + 1) * LANES], bdv, preferred_element_type=F32)
                o[nb, 2 * p] = wq[nb, 2 * p][c:] + av[:, 0:C_DV]
                o[nb, 2 * p + 1] = wq[nb, 2 * p + 1][c:] + av[:, C_DV:]
        for nb, h in chains:
            state_ref[nb * C_V_HEADS + h] = s_new[nb, h]
            y_ref[nb, rows, hsl[h]] = (_rms(o[nb, h], cn) * zs_ref[nb, rows, hsl[h]].astype(F32)).astype(BF16)
        return carry

    lax.fori_loop(0, GDN_NCH, recur, 0)


def _gdn(uc, sm, a_log, dt_bias, c_norm, bsz, seq):
    c, tt = C_CHUNK, GDN_TT
    nt = seq // tt
    pad = lambda v: jnp.pad(v.astype(F32), (0, LANES - v.shape[0])).reshape(1, LANES)
    li = np.arange(LANES)[:, None]
    cj = np.arange(2048)[None, :]
    lp = li % 16
    e = jnp.asarray((li < 48) & (((lp < 8) & (cj < 1024) & (cj // LANES == lp))
                                 | ((lp >= 8) & (cj >= 1024) & ((cj - 1024) // LANES == lp - 8))), BF16)
    ti = np.arange(tt)
    lbd = jnp.asarray((ti[:, None] // c == ti[None, :] // c) & (ti[:, None] >= ti[None, :]), BF16)
    nbat = GDN_NBAT if bsz % GDN_NBAT == 0 else 1
    uc3 = uc.reshape(bsz, seq, UC_W)
    sm3 = sm.reshape(bsz, seq, SM_W)
    row = lambda blockcol: (lambda b, t: (b, t, blockcol))
    const = lambda b, t: (0, 0)
    yc = pl.pallas_call(
        _gdn_kernel,
        grid=(bsz // nbat, nt),
        in_specs=[
            pl.BlockSpec((nbat, tt, 512), row(0)),
            pl.BlockSpec((nbat, tt, 512), row(1)),
            pl.BlockSpec((nbat, tt, 1024), row(1)),
            pl.BlockSpec((nbat, tt, 1024), row(2)),
            pl.BlockSpec((nbat, tt, SM_W), row(0)),
            pl.BlockSpec((1, LANES), const),
            pl.BlockSpec((1, LANES), const),
            pl.BlockSpec((1, LANES), const),
            pl.BlockSpec((LANES, 2048), const),
            pl.BlockSpec((tt, tt), const),
        ],
        out_specs=pl.BlockSpec((nbat, tt, 1024), row(0)),
        out_shape=jax.ShapeDtypeStruct((bsz, seq, 1024), BF16),
        scratch_shapes=[
            pltpu.VMEM((nbat * C_V_HEADS, C_DK, C_DV), F32),
            pltpu.VMEM((nbat, tt, 2048), F32),
            pltpu.VMEM((nbat, tt, 1024), F32),
            pltpu.VMEM((nbat, tt, 1024), BF16),
            pltpu.VMEM((nbat, tt, 1024), BF16),
            pltpu.VMEM((nbat, tt, 1024), BF16),
            pltpu.VMEM((nbat, tt, 512), BF16),
        ],
        compiler_params=_cparams(("parallel", "arbitrary")),
        name="gdn",
    )(uc3, uc3, uc3, uc3, sm3, pad(a_log), pad(dt_bias), c_norm.astype(F32).reshape(1, LANES), e, lbd)
    return yc.reshape(bsz * seq, 1024)


def _merge_kernel(x_ref, o1_ref, o2_ref, o3_ref, l1_ref, l2_ref, l3_ref, yb_ref, yc_ref, ug_ref,
                  wa_ref, wb_ref, wc_ref, wo_ref, out_ref, nat_ref):
    tm = x_ref.shape[0]
    for slot, (dil, src) in enumerate(((4, o2_ref), (4, l2_ref), (16, o3_ref), (16, l3_ref))):
        for r in range(dil):
            for g in range(4):
                col = r * 512 + g * LANES
                nat_ref[slot, g, pl.ds(r, tm // dil, stride=dil), :] = src[:, col:col + LANES].astype(F32)
    parts = []
    for g in range(4):
        sl = slice(g * LANES, (g + 1) * LANES)
        l1, l2, l3 = l1_ref[:, sl], nat_ref[1, g], nat_ref[3, g]
        m = jnp.maximum(jnp.maximum(l1, l2), l3)
        e1, e2, e3 = jnp.exp(l1 - m), jnp.exp(l2 - m), jnp.exp(l3 - m)
        parts.append(((e1 * o1_ref[:, sl].astype(F32) + e2 * nat_ref[0, g] + e3 * nat_ref[2, g])
                      / (e1 + e2 + e3)).astype(BF16))
    ya = jnp.concatenate(parts, axis=1)
    merged = (ug_ref[:, 0:1024].astype(F32) * jnp.dot(ya, wa_ref[...], preferred_element_type=F32)
              + ug_ref[:, 1024:2048].astype(F32) * jnp.dot(yb_ref[...], wb_ref[...], preferred_element_type=F32)
              + ug_ref[:, 2048:3072].astype(F32) * jnp.dot(yc_ref[...], wc_ref[...], preferred_element_type=F32))
    out_ref[...] = x_ref[...] + jnp.dot(merged.astype(BF16), wo_ref[...], preferred_element_type=F32)


def _merge(x2, o_list, lse_list, yb, yc, ug, wa, wb, wc, wo, tm):
    m = x2.shape[0]
    row = lambda i: (i, 0)
    const = lambda i: (0, 0)
    rs = lambda w, dil=1: pl.BlockSpec((tm // dil, dil * w), row)
    return pl.pallas_call(
        _merge_kernel,
        grid=(m // tm,),
        in_specs=[rs(D_MODEL), rs(512), rs(512, 4), rs(512, 16), rs(512), rs(512, 4), rs(512, 16),
                  rs(512), rs(1024), rs(UG_W),
                  pl.BlockSpec((512, D_MODEL), const), pl.BlockSpec((512, D_MODEL), const),
                  pl.BlockSpec((1024, D_MODEL), const), pl.BlockSpec((D_MODEL, D_MODEL), const)],
        out_specs=rs(D_MODEL),
        out_shape=jax.ShapeDtypeStruct((m, D_MODEL), F32),
        scratch_shapes=[pltpu.VMEM((4, 4, tm, LANES), F32)],
        compiler_params=_cparams(("parallel",)),
        name="merge",
    )(x2, *o_list, *lse_list, yb, yc, ug, wa, wb, wc, wo)


def _ffn_kernel(x_ref, g_ref, w1_ref, w2_ref, gf_ref, out_ref, *, final_norm):
    x = x_ref[...]
    h = _rms(x, g_ref[...]).astype(BF16)
    acc = x
    for c in range(D_FF // 1024):
        sl = slice(c * 1024, (c + 1) * 1024)
        a = jnp.maximum(jnp.dot(h, w1_ref[:, sl], preferred_element_type=F32), 0.0)
        acc = acc + jnp.dot((a * a).astype(BF16), w2_ref[sl, :], preferred_element_type=F32)
    out_ref[...] = _rms(acc, gf_ref[...]) if final_norm else acc


def _ffn(x2, gain, w1, w2, gain_final, final_norm, tm):
    m = x2.shape[0]
    row = lambda i: (i, 0)
    const = lambda i: (0, 0)
    return pl.pallas_call(
        functools.partial(_ffn_kernel, final_norm=final_norm),
        grid=(m // tm,),
        in_specs=[pl.BlockSpec((tm, D_MODEL), row), pl.BlockSpec((1, D_MODEL), const),
                  pl.BlockSpec((D_MODEL, D_FF), const), pl.BlockSpec((D_FF, D_MODEL), const),
                  pl.BlockSpec((1, D_MODEL), const)],
        out_specs=pl.BlockSpec((tm, D_MODEL), row),
        out_shape=jax.ShapeDtypeStruct((m, D_MODEL), F32),
        compiler_params=_cparams(("parallel",)),
        name="ffn",
    )(x2, gain, w1, w2, gain_final)


def _rope_tables(positions):
    half = ROT_DIM // 2
    inv_freq = jnp.power(ROPE_THETA, -jnp.arange(0, ROT_DIM, 2, dtype=F32) / ROT_DIM)
    rest = jnp.zeros((HEAD_DIM - ROT_DIM,), F32)
    zero8 = jnp.zeros((half,), F32)
    two = lambda t: jnp.concatenate([t, t])
    freq = two(jnp.concatenate([inv_freq, inv_freq, rest]))
    m1 = two(jnp.concatenate([-jnp.ones((half,), F32), zero8, rest]))
    m2 = two(jnp.concatenate([zero8, jnp.ones((half,), F32), rest]))
    ang = positions.astype(F32).reshape(-1, 1) * freq
    sin = jnp.sin(ang)
    return jnp.cos(ang), sin * m1, sin * m2


def kernel(x, positions, norm_mix, w_in, b_in, conv_w, a_log, dt_bias, sinks, c_norm, w_branch_a, w_branch_b,
           w_branch_c, w_out, norm_ffn, w_ff1, w_ff2, norm_final):
    bsz, seq, d = x.shape
    depth = w_in.shape[0]
    assert d == D_MODEL and seq % (16 * BLK) == 0
    m = bsz * seq
    cosv, s1, s2 = _rope_tables(positions)
    x2 = x.reshape(m, d)
    ca = 5376
    for layer in range(depth):
        wl, bl = w_in[layer], b_in[layer]
        w_main = wl[:, :ca].astype(BF16)
        w_gate = wl[:, ca + 16:].astype(BF16)
        w_small = jnp.pad(wl[:, ca:ca + 16], ((0, 0), (0, SM_W - 16))).astype(BF16)
        b_all = jnp.concatenate([bl[:ca], bl[ca + 16:], bl[ca:ca + 16], jnp.zeros((SM_W - 16,), F32)]).reshape(1, -1)
        ua, ua4, ua16, ub, uc, ug, sm = _inproj(x2, norm_mix[layer].reshape(1, d), w_main, w_gate, w_small, b_all,
                                                cosv, s1, s2, conv_w[layer].astype(F32), seq, tm=512)
        o_list, lse_list = [], []
        for (window, dil), ua_d in zip(A_CONFIGS, (ua, ua4, ua16)):
            o, lse = _attn_a(ua_d, bsz, seq, window, dil)
            o_list.append(o)
            lse_list.append(lse)
        yb = _attn_b(ub, sinks[layer].astype(F32), bsz, seq)
        yc = _gdn(uc, sm, a_log[layer], dt_bias[layer], c_norm[layer], bsz, seq)
        x2 = _merge(x2, o_list, lse_list, yb, yc, ug, w_branch_a[layer].astype(BF16), w_branch_b[layer].astype(BF16),
                    w_branch_c[layer].astype(BF16), w_out[layer].astype(BF16), tm=512)
        x2 = _ffn(x2, norm_ffn[layer].reshape(1, d), w_ff1[layer].astype(BF16), w_ff2[layer].astype(BF16),
                  norm_final.reshape(1, d), final_norm=(layer == depth - 1), tm=512)
    return x2.reshape(bsz, seq, d)
```
